```python
import math
import jax, jax.numpy as jnp
from jax import lax
import numpy as np


D_MODEL = 1024
BATCH = 8
SEQ = 2048
DEPTH = 2
DEC_BATCH = 32
DEC_SEQ = 1
PAST_LEN = 16384
PAGE_SIZE = 128

HEAD_DIM = 64
H_FOX = 8
H_SB = 8
W_FOX = H_FOX * HEAD_DIM
W_SB = H_SB * HEAD_DIM
SSM_GROUP = 16
SSM_GROUPS = 32
SSM_STATE = 64
W_SSM = SSM_GROUP * SSM_GROUPS
Q_BLOCK = 128
EPS = 1e-6
NEG_INF = -1e30
IN_SIZES = (W_FOX, W_FOX, W_FOX, H_FOX, W_FOX,
            W_SB, W_SB, W_SB, W_SB,
            W_SSM, W_SSM,
            D_MODEL, D_MODEL, D_MODEL)
N_IN = 4 * W_FOX + H_FOX + 4 * W_SB + 2 * W_SSM + 3 * D_MODEL

kernel_name = 'hybrid_fox_s5_stickbreak_step'


def _rmsnorm(x, g):
    xf = x.astype(jnp.float32)
    y = xf * lax.rsqrt(jnp.mean(xf * xf, axis=-1, keepdims=True) + EPS)
    return (y * g.astype(jnp.float32)).astype(x.dtype)


def _split_cols(cols):
    offs = []
    acc = 0
    for s in IN_SIZES[:-1]:
        acc += s
        offs.append(acc)
    return jnp.split(cols, offs, axis=-1)


def _in_proj(x, norm_g, w_in, b_f, qn_g, kn_g):
    b, t, _ = x.shape
    h = _rmsnorm(x, norm_g)
    (fq, fk, fv, ff, fz, sq, sk, sv, sz, su, mz, gf, gs, gm) = _split_cols(h @ w_in)
    hd = lambda a: a.reshape(b, t, -1, HEAD_DIM)
    return dict(
        fox_q=_rmsnorm(hd(fq), qn_g), fox_k=_rmsnorm(hd(fk), kn_g), fox_v=hd(fv),
        fox_logf=jax.nn.log_sigmoid((ff + b_f).astype(jnp.float32)),
        fox_z=fz, sb_q=hd(sq), sb_k=hd(sk), sb_v=hd(sv), sb_z=sz,
        ssm_u=su, ssm_z=mz, g_fox=gf, g_sb=gs, g_ssm=gm)


def _fox_cum(logf):
    return logf - lax.cumsum(logf, axis=1, reverse=True)


def _scores(q, ks):
    s = [jnp.einsum('bqhd,bkhd->bhqk', q, k, preferred_element_type=jnp.float32) for k in ks]
    return jnp.concatenate(s, axis=-1) * (HEAD_DIM ** -0.5)


def _mix_values(w, vs):
    out = None
    start = 0
    for v in vs:
        n = v.shape[1]
        o = jnp.einsum('bhqk,bkhd->bqhd', w[..., start:start + n].astype(v.dtype), v)
        out = o if out is None else out + o
        start += n
    return out


def _fox_attend(q, ks, vs, cq, ck, qpos, kpos):
    s = _scores(q, ks)
    s = s + jnp.swapaxes(cq, 1, 2)[..., :, None] - jnp.swapaxes(ck, 1, 2)[..., None, :]
    s = jnp.where(kpos[None, :] <= qpos[:, None], s, NEG_INF)
    return _mix_values(jax.nn.softmax(s, axis=-1), vs)


def _sb_attend(q, ks, vs, qpos, kpos):
    z = _scores(q, ks)
    mask = kpos[None, :] < qpos[:, None]
    log_keep = jnp.where(mask, jax.nn.log_sigmoid(-z), 0.0)
    key_axis = log_keep.ndim - 1
    log_pass = lax.cumsum(log_keep, axis=key_axis, reverse=True) - log_keep
    w = jnp.where(mask, jnp.exp(jax.nn.log_sigmoid(z) + log_pass), 0.0)
    return _mix_values(w, vs)


def _sweep(fn, q_args, qpos):
    t = qpos.shape[0]
    nb = t // Q_BLOCK
    to_blocks = lambda a: jnp.moveaxis(a.reshape(a.shape[0], nb, Q_BLOCK, *a.shape[2:]), 1, 0)
    blocks = tuple(to_blocks(a) for a in q_args) + (qpos.reshape(nb, Q_BLOCK),)
    out = jnp.moveaxis(lax.map(lambda args: fn(*args), blocks), 0, 1)
    return out.reshape(out.shape[0], t, *out.shape[3:])


def _ssm(u, h0_re, h0_im, p):
    f32 = jnp.float32
    b, t, _ = u.shape
    a_re = p['a_re'].astype(f32)
    a_im = p['a_im'].astype(f32)
    dt = jnp.exp(p['log_dt'].astype(f32))[:, None]
    mag = jnp.exp(a_re * dt)
    ang = a_im * dt
    lb_re = mag * jnp.cos(ang)
    lb_im = mag * jnp.sin(ang)
    den = a_re * a_re + a_im * a_im
    nr = lb_re - 1.0
    coef_re = (nr * a_re + lb_im * a_im) / den
    coef_im = (lb_im * a_re - nr * a_im) / den
    ug = u.astype(f32).reshape(b, t, SSM_GROUPS, SSM_GROUP)
    bu_re = jnp.einsum('btgp,gnp->btgn', ug, p['b_re'].astype(f32))
    bu_im = jnp.einsum('btgp,gnp->btgn', ug, p['b_im'].astype(f32))
    x_re = coef_re * bu_re - coef_im * bu_im
    x_im = coef_re * bu_im + coef_im * bu_re
    al_re = jnp.broadcast_to(lb_re, x_re.shape)
    al_im = jnp.broadcast_to(lb_im, x_re.shape)

    def combine(e1, e2):
        a1r, a1i, b1r, b1i = e1
        a2r, a2i, b2r, b2i = e2
        return (a1r * a2r - a1i * a2i, a1r * a2i + a1i * a2r,
                a2r * b1r - a2i * b1i + b2r, a2r * b1i + a2i * b1r + b2i)

    ar, ai, hr, hi = lax.associative_scan(combine, (al_re, al_im, x_re, x_im), axis=1)
    r0 = h0_re[:, None]
    i0 = h0_im[:, None]
    hr = hr + ar * r0 - ai * i0
    hi = hi + ar * i0 + ai * r0
    y = (jnp.einsum('gpn,btgn->btgp', p['c_re'].astype(f32), hr)
         - jnp.einsum('gpn,btgn->btgp', p['c_im'].astype(f32), hi))
    y = y.reshape(b, t, W_SSM) + p['ssm_d'].astype(f32) * u.astype(f32)
    y = jax.nn.gelu(y)
    y = y * jax.nn.sigmoid(y @ p['w_glu'].astype(f32))
    return y.astype(u.dtype), hr[:, -1], hi[:, -1]


def _layer(x, p, past):
    b, t, _ = x.shape
    pr = _in_proj(x, p['norm_g'], p['w_in'], p['b_f'], p['qn_g'], p['kn_g'])
    if past is None:
        n_past = 0
        fox_ks, fox_vs = (pr['fox_k'],), (pr['fox_v'],)
        sb_ks, sb_vs = (pr['sb_k'],), (pr['sb_v'],)
        logf_all = pr['fox_logf']
        h0_re = jnp.zeros((b, SSM_GROUPS, SSM_STATE), jnp.float32)
        h0_im = jnp.zeros((b, SSM_GROUPS, SSM_STATE), jnp.float32)
    else:
        n_past = past['fox_k'].shape[1]
        fox_ks, fox_vs = (past['fox_k'], pr['fox_k']), (past['fox_v'], pr['fox_v'])
        sb_ks, sb_vs = (past['sb_k'], pr['sb_k']), (past['sb_v'], pr['sb_v'])
        logf_all = jnp.concatenate([past['fox_logf'].astype(jnp.float32), pr['fox_logf']], axis=1)
        h0_re = past['ssm_re'].astype(jnp.float32)
        h0_im = past['ssm_im'].astype(jnp.float32)
    c_all = _fox_cum(logf_all)
    cq = c_all[:, n_past:]
    kpos = jnp.arange(n_past + t)
    qpos = n_past + jnp.arange(t)
    fox_fn = lambda q, c, qp: _fox_attend(q, fox_ks, fox_vs, c, c_all, qp, kpos)
    sb_fn = lambda q, qp: _sb_attend(q, sb_ks, sb_vs, qp, kpos)
    if past is None:
        o_fox = _sweep(fox_fn, (pr['fox_q'], cq), qpos)
        o_sb = _sweep(sb_fn, (pr['sb_q'],), qpos)
    else:
        o_fox = fox_fn(pr['fox_q'], cq, qpos)
        o_sb = sb_fn(pr['sb_q'], qpos)
    o_ssm, hr, hi = _ssm(pr['ssm_u'], h0_re, h0_im, p)
    o_fox = o_fox.reshape(b, t, W_FOX) * jax.nn.silu(pr['fox_z'])
    o_sb = o_sb.reshape(b, t, W_SB) * jax.nn.silu(pr['sb_z'])
    o_ssm = o_ssm * jax.nn.silu(pr['ssm_z'])
    gate = lambda g: jax.nn.sigmoid(g.astype(jnp.float32)).astype(x.dtype)
    m = (gate(pr['g_fox']) * (o_fox @ p['w_br_fox'])
         + gate(pr['g_sb']) * (o_sb @ p['w_br_sb'])
         + gate(pr['g_ssm']) * (o_ssm @ p['w_br_ssm']))
    y = x + m @ p['w_out']
    entries = (pr['fox_k'], pr['fox_v'], pr['fox_logf'].astype(x.dtype), pr['sb_k'], pr['sb_v'],
               hr.astype(x.dtype), hi.astype(x.dtype))
    return y, entries


def _gather_pages(pool, l, page_table):
    g = pool[l][page_table]
    return g.reshape(g.shape[0], g.shape[1] * g.shape[2], *g.shape[3:])


def setup_inputs(seed: int = 0) -> dict:
    key = jax.random.key(seed)
    keys = jax.random.split(key, 40)
    counter = [0]

    def nk():
        k = keys[counter[0]]
        counter[0] += 1
        return k

    f32 = jnp.float32
    nrm = lambda shape, scale=1.0: scale * jax.random.normal(nk(), shape, f32)
    n_pages = PAST_LEN // PAGE_SIZE
    n_used = DEC_BATCH * n_pages
    n_pool = n_used + n_used // 4 + 1
    perm = jax.random.permutation(nk(), n_pool)
    page_table = perm[:n_used].reshape(DEC_BATCH, n_pages).astype(jnp.int32)
    kv_shape = (DEPTH, n_pool, PAGE_SIZE, H_FOX, HEAD_DIM)
    sb_shape = (DEPTH, n_pool, PAGE_SIZE, H_SB, HEAD_DIM)
    st_shape = (DEPTH, DEC_BATCH, SSM_GROUPS, SSM_STATE)
    inputs = dict(
        x_prompt=nrm((BATCH, SEQ, D_MODEL)),
        x_sample=nrm((DEC_BATCH, DEC_SEQ, D_MODEL)),
        cache_fox_k=nrm(kv_shape),
        cache_fox_v=nrm(kv_shape),
        cache_fox_logf=jax.nn.log_sigmoid(3.0 + nrm((DEPTH, n_pool, PAGE_SIZE, H_FOX), 0.5)),
        cache_sb_k=nrm(sb_shape),
        cache_sb_v=nrm(sb_shape),
        state_ssm_re=nrm(st_shape, 0.5),
        state_ssm_im=nrm(st_shape, 0.5),
        page_table=page_table,
        norm_g=1.0 + nrm((DEPTH, D_MODEL), 0.02),
        w_in=nrm((DEPTH, D_MODEL, N_IN), D_MODEL ** -0.5),
        b_fgate=3.0 + nrm((DEPTH, H_FOX), 0.5),
        qn_g=1.0 + nrm((DEPTH, HEAD_DIM), 0.02),
        kn_g=1.0 + nrm((DEPTH, HEAD_DIM), 0.02),
        ssm_a_re=-0.5 + nrm((DEPTH, SSM_GROUPS, SSM_STATE), 0.01),
        ssm_a_im=jnp.pi * jnp.arange(SSM_STATE, dtype=f32) + nrm((DEPTH, SSM_GROUPS, SSM_STATE), 0.01),
        ssm_log_dt=jax.random.uniform(nk(), (DEPTH, SSM_GROUPS), f32, math.log(1e-3), math.log(1e-1)),
        ssm_b_re=nrm((DEPTH, SSM_GROUPS, SSM_STATE, SSM_GROUP), (2 * SSM_GROUP) ** -0.5),
        ssm_b_im=nrm((DEPTH, SSM_GROUPS, SSM_STATE, SSM_GROUP), (2 * SSM_GROUP) ** -0.5),
        ssm_c_re=nrm((DEPTH, SSM_GROUPS, SSM_GROUP, SSM_STATE), (2 * SSM_STATE) ** -0.5),
        ssm_c_im=nrm((DEPTH, SSM_GROUPS, SSM_GROUP, SSM_STATE), (2 * SSM_STATE) ** -0.5),
        ssm_d=nrm((DEPTH, W_SSM)),
        w_glu=nrm((DEPTH, W_SSM, W_SSM), W_SSM ** -0.5),
        w_br_fox=nrm((DEPTH, W_FOX, D_MODEL), W_FOX ** -0.5),
        w_br_sb=nrm((DEPTH, W_SB, D_MODEL), W_SB ** -0.5),
        w_br_ssm=nrm((DEPTH, W_SSM, D_MODEL), W_SSM ** -0.5),
        w_out=nrm((DEPTH, D_MODEL, D_MODEL), D_MODEL ** -0.5),
    )
    return inputs


def reference(x_prompt, x_sample, cache_fox_k, cache_fox_v, cache_fox_logf, cache_sb_k, cache_sb_v,
              state_ssm_re, state_ssm_im, page_table, norm_g, w_in, b_fgate, qn_g, kn_g,
              ssm_a_re, ssm_a_im, ssm_log_dt, ssm_b_re, ssm_b_im, ssm_c_re, ssm_c_im, ssm_d, w_glu,
              w_br_fox, w_br_sb, w_br_ssm, w_out):
    yp = x_prompt
    ys = x_sample
    ents_p = []
    ents_s = []
    for l in range(DEPTH):
        p = dict(norm_g=norm_g[l], w_in=w_in[l], b_f=b_fgate[l], qn_g=qn_g[l], kn_g=kn_g[l],
                 a_re=ssm_a_re[l], a_im=ssm_a_im[l], log_dt=ssm_log_dt[l],
                 b_re=ssm_b_re[l], b_im=ssm_b_im[l], c_re=ssm_c_re[l], c_im=ssm_c_im[l],
                 ssm_d=ssm_d[l], w_glu=w_glu[l], w_br_fox=w_br_fox[l], w_br_sb=w_br_sb[l],
                 w_br_ssm=w_br_ssm[l], w_out=w_out[l])
        yp, ent = _layer(yp, p, None)
        ents_p.append(ent)
        past = dict(fox_k=_gather_pages(cache_fox_k, l, page_table),
                    fox_v=_gather_pages(cache_fox_v, l, page_table),
                    fox_logf=_gather_pages(cache_fox_logf, l, page_table),
                    sb_k=_gather_pages(cache_sb_k, l, page_table),
                    sb_v=_gather_pages(cache_sb_v, l, page_table),
                    ssm_re=state_ssm_re[l], ssm_im=state_ssm_im[l])
        ys, ent = _layer(ys, p, past)
        ents_s.append(ent)
    p_fox_k, p_fox_v, p_fox_logf, p_sb_k, p_sb_v, p_ssm_re, p_ssm_im = [jnp.stack(e) for e in zip(*ents_p)]
    s_fox_k, s_fox_v, s_fox_logf, s_sb_k, s_sb_v, s_ssm_re, s_ssm_im = [jnp.stack(e) for e in zip(*ents_s)]
    return (yp, ys, p_fox_k, p_fox_v, p_fox_logf, p_sb_k, p_sb_v, p_ssm_re, p_ssm_im,
            s_fox_k, s_fox_v, s_fox_logf, s_sb_k, s_sb_v, s_ssm_re, s_ssm_im)
```

```python
import functools
import math

import jax
import jax.numpy as jnp
from jax import lax
from jax.experimental import pallas as pl
from jax.experimental.pallas import tpu as pltpu

HEAD_DIM = 64
N_HEADS = 8
W_ATT = N_HEADS * HEAD_DIM
SSM_GROUP = 16
SSM_GROUPS = 32
SSM_STATE = 64
W_SSM = SSM_GROUP * SSM_GROUPS
N_STATE = SSM_GROUPS * SSM_STATE
EPS = 1e-6
NEG_INF = -1e30
SCALE = HEAD_DIM ** -0.5

LANES = 128
V7X_VMEM_BYTES = 64 * 1024 * 1024
VMEM_LIMIT = V7X_VMEM_BYTES - 8 * 1024 * 1024

F32 = jnp.float32
BF16 = jnp.bfloat16


def _cparams(*sem):
    return pltpu.CompilerParams(dimension_semantics=sem, vmem_limit_bytes=VMEM_LIMIT)


def _dot_nn(a, b, precision=None):
    return lax.dot_general(a, b, (((1,), (0,)), ((), ())), precision=precision,
                           preferred_element_type=F32)


def _dot_nt(a, b):
    return lax.dot_general(a, b, (((1,), (1,)), ((), ())), preferred_element_type=F32)


def _dot_tn(a, b):
    return lax.dot_general(a, b, (((0,), (0,)), ((), ())), preferred_element_type=F32)


def _log_sigmoid(x):
    return jnp.minimum(x, 0.0) - jnp.log1p(jnp.exp(-jnp.abs(x)))


def _softplus(x):
    return jnp.maximum(x, 0.0) + jnp.log1p(jnp.exp(-jnp.abs(x)))


def _sigmoid(x):
    return jax.nn.sigmoid(x)


def _gelu_tanh(x):
    c = math.sqrt(2.0 / math.pi)
    return x * (0.5 * (1.0 + jnp.tanh(c * (x + 0.044715 * (x * x * x)))))


def _rms_rows(x, g):
    ms = jnp.mean(x * x, axis=-1, keepdims=True)
    return x * lax.rsqrt(ms + EPS) * g


def _headnorm_t(xt, g_col):
    n = xt.shape[-1]
    x3 = xt.reshape(N_HEADS, HEAD_DIM, n)
    ms = jnp.mean(x3 * x3, axis=1, keepdims=True)
    y = x3 * lax.rsqrt(ms + EPS) * g_col[None]
    return y.reshape(W_ATT, n)


def _inproj_prompt_kernel(x_ref, g_ref, w_ref, wff_ref, bfc_ref, bfr_ref, qn_ref, kn_ref,
                          fq_ref, fk_ref, fv_ref, lf_ref, lfn_ref, sq_ref, sk_ref, sv_ref,
                          u_ref):
    h = _rms_rows(x_ref[...], g_ref[...]).astype(BF16)

    def proj_t(i):
        return _dot_nt(w_ref[i * W_ATT:(i + 1) * W_ATT, :], h)

    fq_ref[...] = (_headnorm_t(proj_t(0), qn_ref[...]) * SCALE).astype(BF16)
    fk_ref[...] = _headnorm_t(proj_t(1), kn_ref[...])
    fv_ref[...] = proj_t(2)
    wff = wff_ref[...]
    lf_ref[...] = _log_sigmoid(_dot_nt(wff, h)[:N_HEADS, :] + bfc_ref[...])
    lfn_ref[...] = _log_sigmoid(_dot_nt(h, wff)[:, :N_HEADS] + bfr_ref[...])
    sq_ref[...] = (proj_t(3) * SCALE).astype(BF16)
    sk_ref[...] = proj_t(4)
    sv_ref[...] = proj_t(5)
    u_ref[...] = _dot_nt(h, w_ref[6 * W_ATT:7 * W_ATT, :])


def _inproj_prompt(x, g_row, w_qkv, w_ff, bf_col, bf_row, qn_col, kn_col, tm):
    b, t, d = x.shape
    tm = min(tm, t)
    full = lambda a: pl.BlockSpec(a.shape, lambda i, j: (0,) * a.ndim)
    tspec = pl.BlockSpec((None, W_ATT, tm), lambda i, j: (i, 0, j))
    kv = jax.ShapeDtypeStruct((b, W_ATT, t), F32)
    qs = jax.ShapeDtypeStruct((b, W_ATT, t), BF16)
    return pl.pallas_call(
        _inproj_prompt_kernel,
        grid=(b, t // tm),
        in_specs=[pl.BlockSpec((None, tm, d), lambda i, j: (i, j, 0)), full(g_row), full(w_qkv),
                  full(w_ff), full(bf_col), full(bf_row), full(qn_col), full(kn_col)],
        out_specs=[tspec, tspec, tspec,
                   pl.BlockSpec((None, N_HEADS, tm), lambda i, j: (i, 0, j)),
                   pl.BlockSpec((None, tm, N_HEADS), lambda i, j: (i, j, 0)),
                   tspec, tspec, tspec,
                   pl.BlockSpec((None, tm, W_SSM), lambda i, j: (i, j, 0))],
        out_shape=[qs, kv, kv, jax.ShapeDtypeStruct((b, N_HEADS, t), F32),
                   jax.ShapeDtypeStruct((b, t, N_HEADS), F32), qs, kv, kv,
                   jax.ShapeDtypeStruct((b, t, W_SSM), F32)],
        compiler_params=_cparams("parallel", "parallel"),
        name="inproj_prompt",
    )(x, g_row, w_qkv, w_ff, bf_col, bf_row, qn_col, kn_col)


def _fox_prompt_kernel(q_ref, k_ref, v_ref, lf_ref, lfn_ref, o_ref, crow_ref, ccol_ref,
                       *, tq, cb):
    qi = pl.program_id(1)
    t = k_ref.shape[-1]

    @pl.when(qi == 0)
    def _():
        r = lax.broadcasted_iota(jnp.int32, (cb, cb), 0)
        c = lax.broadcasted_iota(jnp.int32, (cb, cb), 1)
        upper = (c > r).astype(F32)
        lower = (r > c).astype(F32)
        carry_c = jnp.zeros((1, N_HEADS), F32)
        carry_r = jnp.zeros((N_HEADS, 1), F32)
        for blk in reversed(range(t // cb)):
            seg_c = lfn_ref[blk * cb:(blk + 1) * cb, :]
            within_c = _dot_nn(upper, seg_c, precision=lax.Precision.HIGHEST)
            ccol_ref[blk * cb:(blk + 1) * cb, :] = -(within_c + carry_c)
            carry_c = carry_c + jnp.sum(seg_c, axis=0, keepdims=True)
            seg_r = lf_ref[:, blk * cb:(blk + 1) * cb]
            within_r = _dot_nn(seg_r, lower, precision=lax.Precision.HIGHEST)
            crow_ref[:, blk * cb:(blk + 1) * cb] = -(within_r + carry_r)
            carry_r = carry_r + jnp.sum(seg_r, axis=1, keepdims=True)

    q0 = pl.multiple_of(qi * tq, tq)
    for h in range(N_HEADS):
        rows = slice(h * HEAD_DIM, (h + 1) * HEAD_DIM)
        qh = q_ref[rows, :]
        cq = crow_ref[h:h + 1, pl.ds(q0, tq)]

        def step(j, carry, masked):
            m, l, acc = carry
            k0 = pl.multiple_of(j * tq, tq)
            kt = k_ref[rows, pl.ds(k0, tq)].astype(BF16)
            s = _dot_tn(kt, qh)
            s = s + cq - ccol_ref[pl.ds(k0, tq), h:h + 1]
            if masked:
                kpos = lax.broadcasted_iota(jnp.int32, (tq, tq), 0)
                qpos = lax.broadcasted_iota(jnp.int32, (tq, tq), 1)
                s = jnp.where(kpos <= qpos, s, NEG_INF)
            m_new = jnp.maximum(m, jnp.max(s, axis=0, keepdims=True))
            p = jnp.exp(s - m_new)
            alpha = jnp.exp(m - m_new)
            l = alpha * l + jnp.sum(p, axis=0, keepdims=True)
            vt = v_ref[rows, pl.ds(k0, tq)].astype(BF16)
            acc = alpha * acc + _dot_nn(vt, p.astype(BF16))
            return m_new, l, acc

        init = (jnp.full((1, tq), NEG_INF, F32), jnp.zeros((1, tq), F32),
                jnp.zeros((HEAD_DIM, tq), F32))
        carry = lax.fori_loop(0, qi, functools.partial(step, masked=False), init)
        _, l, acc = step(qi, carry, True)
        o_ref[rows, :] = acc / l


def _fox_prompt(q_t, k_t, v_t, lf_t, lf_n, tq):
    b, _, t = k_t.shape
    tq = min(tq, t)
    whole = pl.BlockSpec((None, W_ATT, t), lambda i, j: (i, 0, 0))
    return pl.pallas_call(
        functools.partial(_fox_prompt_kernel, tq=tq, cb=min(256, t)),
        grid=(b, t // tq),
        in_specs=[pl.BlockSpec((None, W_ATT, tq), lambda i, j: (i, 0, j)), whole, whole,
                  pl.BlockSpec((None, N_HEADS, t), lambda i, j: (i, 0, 0)),
                  pl.BlockSpec((None, t, N_HEADS), lambda i, j: (i, 0, 0))],
        out_specs=pl.BlockSpec((None, W_ATT, tq), lambda i, j: (i, 0, j)),
        out_shape=jax.ShapeDtypeStruct((b, W_ATT, t), F32),
        scratch_shapes=[pltpu.VMEM((N_HEADS, t), F32), pltpu.VMEM((t, N_HEADS), F32)],
        compiler_params=_cparams("parallel", "arbitrary"),
        name="fox_prompt",
    )(q_t, k_t, v_t, lf_t, lf_n)


def _sb_prompt_kernel(q_ref, k_ref, v_ref, o_ref, *, tq, tk):
    qi = pl.program_id(1)
    q0 = qi * tq
    nk = tq // tk
    r = lax.broadcasted_iota(jnp.int32, (tk, tk), 0)
    c = lax.broadcasted_iota(jnp.int32, (tk, tk), 1)
    upper = (c > r).astype(BF16)

    for h in range(N_HEADS):
        rows = slice(h * HEAD_DIM, (h + 1) * HEAD_DIM)
        qh = q_ref[rows, :]

        def step(j, carry, masked):
            run, acc = carry
            kj = qi * nk + (nk - 1) - j
            k0 = pl.multiple_of(kj * tk, tk)
            kt = k_ref[rows, pl.ds(k0, tk)].astype(BF16)
            z = _dot_tn(kt, qh)
            sp = _softplus(z)
            if masked:
                kpos = k0 + lax.broadcasted_iota(jnp.int32, (tk, tq), 0)
                qpos = q0 + lax.broadcasted_iota(jnp.int32, (tk, tq), 1)
                mask = kpos < qpos
                lk = jnp.where(mask, -sp, 0.0)
            else:
                lk = -sp
            lk_hi = lk.astype(BF16)
            lk_lo = (lk - lk_hi.astype(F32)).astype(BF16)
            within = _dot_nn(upper, lk_hi) + _dot_nn(upper, lk_lo)
            w = jnp.exp((z - sp) + (within + run))
            if masked:
                w = jnp.where(mask, w, 0.0)
            vt = v_ref[rows, pl.ds(k0, tk)].astype(BF16)
            acc = acc + _dot_nn(vt, w.astype(BF16))
            run = run + jnp.sum(lk, axis=0, keepdims=True)
            return run, acc

        carry = (jnp.zeros((1, tq), F32), jnp.zeros((HEAD_DIM, tq), F32))
        for j in range(nk):
            carry = step(j, carry, True)
        carry = lax.fori_loop(nk, (qi + 1) * nk, functools.partial(step, masked=False), carry)
        o_ref[rows, :] = carry[1]


def _sb_prompt(q_t, k_t, v_t, tq, tk):
    b, _, t = k_t.shape
    tq = min(tq, t)
    tk = min(tk, tq)
    whole = pl.BlockSpec((None, W_ATT, t), lambda i, j: (i, 0, 0))
    return pl.pallas_call(
        functools.partial(_sb_prompt_kernel, tq=tq, tk=tk),
        grid=(b, t // tq),
        in_specs=[pl.BlockSpec((None, W_ATT, tq), lambda i, j: (i, 0, j)), whole, whole],
        out_specs=pl.BlockSpec((None, W_ATT, tq), lambda i, j: (i, 0, j)),
        out_shape=jax.ShapeDtypeStruct((b, W_ATT, t), F32),
        compiler_params=_cparams("parallel", "parallel"),
        name="sb_prompt",
    )(q_t, k_t, v_t)


def _ssm_discretise(a_re, a_im, log_dt):
    dt = jnp.exp(log_dt)
    mag = jnp.exp(a_re * dt)
    ang = a_im * dt
    lb_re = mag * jnp.cos(ang)
    lb_im = mag * jnp.sin(ang)
    den = a_re * a_re + a_im * a_im
    nr = lb_re - 1.0
    coef_re = (nr * a_re + lb_im * a_im) / den
    coef_im = (lb_im * a_re - nr * a_im) / den
    return lb_re, lb_im, coef_re, coef_im


def _ssm_readout(h_re, h_im, u, cre_ref, cim_ref, d_row, wglu_ref):
    half_s, half_w = N_STATE // 2, W_SSM // 2
    hr = h_re.astype(BF16)
    hi = h_im.astype(BF16)
    y_lo = (_dot_nn(hr[:, :half_s], cre_ref[:half_s, :half_w])
            - _dot_nn(hi[:, :half_s], cim_ref[:half_s, :half_w]))
    y_hi = (_dot_nn(hr[:, half_s:], cre_ref[half_s:, half_w:])
            - _dot_nn(hi[:, half_s:], cim_ref[half_s:, half_w:]))
    y = jnp.concatenate([y_lo, y_hi], axis=1) + d_row * u
    y = _gelu_tanh(y)
    return y * _sigmoid(_dot_nn(y.astype(BF16), wglu_ref[...]))


def _ssm_prompt_kernel(u_ref, are_ref, aim_ref, ldt_ref, bre_ref, bim_ref, cre_ref, cim_ref,
                       d_ref, wglu_ref, o_ref, hre_ref, him_ref,
                       xre_ref, xim_ref, par_ref, hst_ref, *, nb, tt, lane_chunk):
    i = pl.program_id(0)

    @pl.when(i == 0)
    def _():
        pars = _ssm_discretise(are_ref[...], aim_ref[...], ldt_ref[...])
        for k in range(4):
            par_ref[k] = jnp.broadcast_to(pars[k], (nb, N_STATE))
        hst_ref[...] = jnp.zeros_like(hst_ref)

    ub = u_ref[...].astype(BF16)
    half_s, half_w = N_STATE // 2, W_SSM // 2
    xre_ref[:, :half_s] = _dot_nn(ub[:, :half_w], bre_ref[:half_w, :half_s])
    xre_ref[:, half_s:] = _dot_nn(ub[:, half_w:], bre_ref[half_w:, half_s:])
    xim_ref[:, :half_s] = _dot_nn(ub[:, :half_w], bim_ref[:half_w, :half_s])
    xim_ref[:, half_s:] = _dot_nn(ub[:, half_w:], bim_ref[half_w:, half_s:])

    for c in range(N_STATE // lane_chunk):
        cols = slice(c * lane_chunk, (c + 1) * lane_chunk)
        lb_re, lb_im = par_ref[0, :, cols], par_ref[1, :, cols]
        cf_re, cf_im = par_ref[2, :, cols], par_ref[3, :, cols]

        def body(t, carry):
            h_re, h_im = carry
            rws = pl.ds(pl.multiple_of(t * nb, nb), nb)
            b_re, b_im = xre_ref[rws, cols], xim_ref[rws, cols]
            x_re = cf_re * b_re - cf_im * b_im
            x_im = cf_re * b_im + cf_im * b_re
            n_re = lb_re * h_re - lb_im * h_im + x_re
            n_im = lb_re * h_im + lb_im * h_re + x_im
            xre_ref[rws, cols] = n_re
            xim_ref[rws, cols] = n_im
            return n_re, n_im

        h_re, h_im = lax.fori_loop(0, tt, body, (hst_ref[0, :, cols], hst_ref[1, :, cols]),
                                   unroll=8)
        hst_ref[0, :, cols] = h_re
        hst_ref[1, :, cols] = h_im

    o_ref[...] = _ssm_readout(xre_ref[...], xim_ref[...], u_ref[...], cre_ref, cim_ref,
                              d_ref[...], wglu_ref)
    hre_ref[...] = hst_ref[0]
    him_ref[...] = hst_ref[1]


def _ssm_prompt(u_tm, nb, a_re, a_im, ldt, bre, bim, cre, cim, d_row, wglu, tt):
    t = u_tm.shape[0] // nb
    tt = min(tt, t)
    full = lambda a: pl.BlockSpec(a.shape, lambda i: (0,) * a.ndim)
    st = jax.ShapeDtypeStruct((nb, N_STATE), F32)
    return pl.pallas_call(
        functools.partial(_ssm_prompt_kernel, nb=nb, tt=tt, lane_chunk=512),
        grid=(t // tt,),
        in_specs=[pl.BlockSpec((nb * tt, W_SSM), lambda i: (i, 0)), full(a_re), full(a_im),
                  full(ldt), full(bre), full(bim), full(cre), full(cim), full(d_row), full(wglu)],
        out_specs=[pl.BlockSpec((nb * tt, W_SSM), lambda i: (i, 0)),
                   pl.BlockSpec((nb, N_STATE), lambda i: (0, 0)),
                   pl.BlockSpec((nb, N_STATE), lambda i: (0, 0))],
        out_shape=[jax.ShapeDtypeStruct((t * nb, W_SSM), F32), st, st],
        scratch_shapes=[pltpu.VMEM((nb * tt, N_STATE), F32), pltpu.VMEM((nb * tt, N_STATE), F32),
                        pltpu.VMEM((4, nb, N_STATE), F32), pltpu.VMEM((2, nb, N_STATE), F32)],
        compiler_params=_cparams("arbitrary"),
        name="ssm_prompt",
    )(u_tm, a_re, a_im, ldt, bre, bim, cre, cim, d_row, wglu)


def _merge_kernel(x_ref, of_ref, os_ref, om_ref, g_ref, wg_ref, wbr_ref, wo_ref, y_ref,
                  *, attn_transposed):
    x = x_ref[...]
    h = _rms_rows(x, g_ref[...]).astype(BF16)

    def proj(lo, n):
        return _dot_nt(h, wg_ref[lo:lo + n, :])

    def branch(o, z_lo, wi, g_lo):
        z = proj(z_lo, W_ATT)
        a = (o * (z * _sigmoid(z))).astype(BF16)
        br = _dot_nn(a, wbr_ref[wi])
        return _sigmoid(proj(g_lo, x.shape[-1])) * br

    of = of_ref[...].T if attn_transposed else of_ref[...]
    os_ = os_ref[...].T if attn_transposed else os_ref[...]
    d = x.shape[-1]
    m = (branch(of, 0, 0, 3 * W_ATT) + branch(os_, W_ATT, 1, 3 * W_ATT + d)
         + branch(om_ref[...], 2 * W_ATT, 2, 3 * W_ATT + 2 * d))
    y_ref[...] = x + _dot_nn(m.astype(BF16), wo_ref[...])


def _merge(x, of, os_, om, g_row, w_gate, w_br, w_out, tm, attn_transposed):
    b, t, d = x.shape
    tm = min(tm, t)
    full = lambda a: pl.BlockSpec(a.shape, lambda i, j: (0,) * a.ndim)
    rspec = lambda n: pl.BlockSpec((None, tm, n), lambda i, j: (i, j, 0))
    aspec = pl.BlockSpec((None, W_ATT, tm), lambda i, j: (i, 0, j)) if attn_transposed \
        else rspec(W_ATT)
    return pl.pallas_call(
        functools.partial(_merge_kernel, attn_transposed=attn_transposed),
        grid=(b, t // tm),
        in_specs=[rspec(d), aspec, aspec, rspec(W_SSM), full(g_row), full(w_gate), full(w_br),
                  full(w_out)],
        out_specs=rspec(d),
        out_shape=jax.ShapeDtypeStruct((b, t, d), F32),
        compiler_params=_cparams("parallel", "parallel"),
        name="merge_t" if attn_transposed else "merge_n",
    )(x, of, os_, om, g_row, w_gate, w_br, w_out)


def _decode_pre_kernel(x_ref, g_ref, w_ref, wff_ref, bfr_ref, qn_ref, kn_ref, hd_ref,
                       are_ref, aim_ref, ldt_ref, bre_ref, bim_ref, cre_ref, cim_ref, d_ref,
                       wglu_ref, h0r_ref, h0i_ref,
                       fq_ref, fk_ref, fv_ref, lf_ref, sq_ref, sk_ref, sv_ref, om_ref,
                       h1r_ref, h1i_ref):
    h = _rms_rows(x_ref[...], g_ref[...]).astype(BF16)

    def proj(i):
        return _dot_nt(h, w_ref[i * W_ATT:(i + 1) * W_ATT, :])

    def headnorm(x, g_row):
        ms = _dot_nn(x * x, hd_ref[...], precision=lax.Precision.HIGHEST)
        return x * lax.rsqrt(ms + EPS) * g_row

    fq_ref[...] = headnorm(proj(0), qn_ref[...]) * SCALE
    fk_ref[...] = headnorm(proj(1), kn_ref[...])
    fv_ref[...] = proj(2)
    lf_ref[...] = _log_sigmoid(_dot_nt(h, wff_ref[...])[:, :N_HEADS] + bfr_ref[...])
    sq_ref[...] = proj(3) * SCALE
    sk_ref[...] = proj(4)
    sv_ref[...] = proj(5)
    u = proj(6)
    lb_re, lb_im, cf_re, cf_im = _ssm_discretise(are_ref[...], aim_ref[...], ldt_ref[...])
    ub = u.astype(BF16)
    b_re = _dot_nn(ub, bre_ref[...])
    b_im = _dot_nn(ub, bim_ref[...])
    x_re = cf_re * b_re - cf_im * b_im
    x_im = cf_re * b_im + cf_im * b_re
    r0, i0 = h0r_ref[...], h0i_ref[...]
    h_re = x_re + lb_re * r0 - lb_im * i0
    h_im = x_im + lb_re * i0 + lb_im * r0
    h1r_ref[...] = h_re
    h1i_ref[...] = h_im
    om_ref[...] = _ssm_readout(h_re, h_im, u, cre_ref, cim_ref, d_ref[...], wglu_ref)


def _decode_pre(x, g_row, w_qkv, w_ff, bf_row, qn_row, kn_row, headmean, a_re, a_im, ldt,
                bre, bim, cre, cim, d_row, wglu, h0r, h0i):
    n = x.shape[0]
    att = jax.ShapeDtypeStruct((n, W_ATT), F32)
    st = jax.ShapeDtypeStruct((n, N_STATE), F32)
    return pl.pallas_call(
        _decode_pre_kernel,
        out_shape=[att, att, att, jax.ShapeDtypeStruct((n, N_HEADS), F32), att, att, att,
                   jax.ShapeDtypeStruct((n, W_SSM), F32), st, st],
        compiler_params=pltpu.CompilerParams(vmem_limit_bytes=VMEM_LIMIT),
        name="decode_pre",
    )(x, g_row, w_qkv, w_ff, bf_row, qn_row, kn_row, headmean, a_re, a_im, ldt, bre, bim,
      cre, cim, d_row, wglu, h0r, h0i)


def _suffix_sum_lanes(x):
    lane = lax.broadcasted_iota(jnp.int32, x.shape, 1)
    y = x
    k = 1
    while k < LANES:
        shifted = pltpu.roll(y, LANES - k, axis=1)
        y = y + jnp.where(lane + k < LANES, shifted, 0.0)
        k *= 2
    return y - x


def _fox_decode_kernel(pt_ref, q_ref, kn_ref, vn_ref, lfn_ref, *refs, g):
    k_refs, v_refs, lf_refs = refs[:g], refs[g:2 * g], refs[2 * g:3 * g]
    o_ref, qb_ref, acc_ref, m_ref, l_ref, ws_ref, run_ref = refs[3 * g:]
    j = pl.program_id(1)

    @pl.when(j == 0)
    def _():
        qb_ref[...] = jnp.broadcast_to(q_ref[...], qb_ref.shape)
        m_ref[...] = jnp.sum(q_ref[...] * kn_ref[...], axis=1)
        l_ref[...] = jnp.ones_like(l_ref)
        ws_ref[...] = jnp.ones_like(ws_ref)
        acc_ref[...] = jnp.zeros_like(acc_ref)
        run_ref[...] = lfn_ref[...]

    for i in range(g):
        lf = lf_refs[i][...]
        s = jnp.sum(k_refs[i][...] * qb_ref[...], axis=1)
        s = s + (run_ref[...] + _suffix_sum_lanes(lf))
        m_old = m_ref[...]
        m_new = jnp.maximum(m_old, jnp.max(s, axis=1, keepdims=True))
        alpha = jnp.exp(m_old - m_new)
        p = jnp.exp(s - m_new)
        l_ref[...] = alpha * l_ref[...] + jnp.sum(p, axis=1, keepdims=True)
        ws_ref[...] = alpha * ws_ref[...]
        acc_ref[...] = alpha[:, :, None] * acc_ref[...] + v_refs[i][...] * p[:, None, :]
        m_ref[...] = m_new
        run_ref[...] = run_ref[...] + jnp.sum(lf, axis=1, keepdims=True)

    @pl.when(j == pl.num_programs(1) - 1)
    def _():
        o = jnp.sum(acc_ref[...], axis=2, keepdims=True) + ws_ref[...][:, :, None] * vn_ref[...]
        o_ref[...] = o / l_ref[...][:, :, None]


def _sb_decode_kernel(pt_ref, q_ref, *refs, g):
    k_refs, v_refs = refs[:g], refs[g:2 * g]
    o_ref, qb_ref, acc_ref, run_ref = refs[2 * g:]
    j = pl.program_id(1)

    @pl.when(j == 0)
    def _():
        qb_ref[...] = jnp.broadcast_to(q_ref[...], qb_ref.shape)
        acc_ref[...] = jnp.zeros_like(acc_ref)
        run_ref[...] = jnp.zeros_like(run_ref)

    for i in range(g):
        z = jnp.sum(k_refs[i][...] * qb_ref[...], axis=1)
        sp = _softplus(z)
        lk = -sp
        w = jnp.exp((z - sp) + (_suffix_sum_lanes(lk) + run_ref[...]))
        acc_ref[...] = acc_ref[...] + v_refs[i][...] * w[:, None, :]
        run_ref[...] = run_ref[...] + jnp.sum(lk, axis=1, keepdims=True)

    @pl.when(j == pl.num_programs(1) - 1)
    def _():
        o_ref[...] = jnp.sum(acc_ref[...], axis=2, keepdims=True)


def _page_spec(layer, n_pages, g, i, tail):
    def index(b, j, pt):
        return (layer, pt[b, n_pages - 1 - (j * g + i)]) + (0,) * len(tail)
    return pl.BlockSpec((None, None) + tail, index)


def _fox_decode(page_table, q, k_new, v_new, lf_new, kc, vc, lfc, layer, g):
    n, n_pages = page_table.shape
    page = kc.shape[-1]
    col = pl.BlockSpec((None, N_HEADS, HEAD_DIM, 1), lambda b, j, pt: (b, 0, 0, 0))
    hcol = pl.BlockSpec((None, N_HEADS, 1), lambda b, j, pt: (b, 0, 0))
    kv_tail = (N_HEADS, HEAD_DIM, page)
    in_specs = ([col, col, col, hcol]
                + [_page_spec(layer, n_pages, g, i, kv_tail) for i in range(g)]
                + [_page_spec(layer, n_pages, g, i, kv_tail) for i in range(g)]
                + [_page_spec(layer, n_pages, g, i, (N_HEADS, page)) for i in range(g)])
    return pl.pallas_call(
        functools.partial(_fox_decode_kernel, g=g),
        grid_spec=pltpu.PrefetchScalarGridSpec(
            num_scalar_prefetch=1, grid=(n, n_pages // g), in_specs=in_specs, out_specs=col,
            scratch_shapes=[pltpu.VMEM((N_HEADS, HEAD_DIM, page), F32),
                            pltpu.VMEM((N_HEADS, HEAD_DIM, page), F32),
                            pltpu.VMEM((N_HEADS, 1), F32), pltpu.VMEM((N_HEADS, 1), F32),
                            pltpu.VMEM((N_HEADS, 1), F32), pltpu.VMEM((N_HEADS, 1), F32)]),
        out_shape=jax.ShapeDtypeStruct((n, N_HEADS, HEAD_DIM, 1), F32),
        compiler_params=_cparams("parallel", "arbitrary"),
        name="fox_decode",
    )(page_table, q, k_new, v_new, lf_new, *([kc] * g), *([vc] * g), *([lfc] * g))


def _sb_decode(page_table, q, kc, vc, layer, g):
    n, n_pages = page_table.shape
    page = kc.shape[-1]
    col = pl.BlockSpec((None, N_HEADS, HEAD_DIM, 1), lambda b, j, pt: (b, 0, 0, 0))
    kv_tail = (N_HEADS, HEAD_DIM, page)
    in_specs = ([col] + [_page_spec(layer, n_pages, g, i, kv_tail) for i in range(g)]
                + [_page_spec(layer, n_pages, g, i, kv_tail) for i in range(g)])
    return pl.pallas_call(
        functools.partial(_sb_decode_kernel, g=g),
        grid_spec=pltpu.PrefetchScalarGridSpec(
            num_scalar_prefetch=1, grid=(n, n_pages // g), in_specs=in_specs, out_specs=col,
            scratch_shapes=[pltpu.VMEM((N_HEADS, HEAD_DIM, page), F32),
                            pltpu.VMEM((N_HEADS, HEAD_DIM, page), F32),
                            pltpu.VMEM((N_HEADS, 1), F32)]),
        out_shape=jax.ShapeDtypeStruct((n, N_HEADS, HEAD_DIM, 1), F32),
        compiler_params=_cparams("parallel", "arbitrary"),
        name="sb_decode",
    )(page_table, q, *([kc] * g), *([vc] * g))


def _block_diag(blocks):
    g, r, c = blocks.shape
    eye = jnp.eye(g, dtype=blocks.dtype)
    return jnp.einsum("grc,gh->grhc", blocks, eye).reshape(g * r, g * c)


def _prep_layer(l, norm_g, w_in, b_fgate, qn_g, kn_g, ssm_a_re, ssm_a_im, ssm_log_dt,
                ssm_b_re, ssm_b_im, ssm_c_re, ssm_c_im, ssm_d, w_glu, w_br_fox, w_br_sb,
                w_br_ssm, w_out):
    d = w_in.shape[1]
    wt = jnp.swapaxes(w_in[l], 0, 1)
    off = [0]
    for s in (W_ATT, W_ATT, W_ATT, N_HEADS, W_ATT, W_ATT, W_ATT, W_ATT, W_ATT, W_SSM, W_SSM,
              d, d, d):
        off.append(off[-1] + s)
    seg = lambda i: wt[off[i]:off[i + 1]]
    w_qkv = jnp.concatenate([seg(0), seg(1), seg(2), seg(5), seg(6), seg(7), seg(9)]).astype(BF16)
    w_ff = jnp.concatenate([seg(3), jnp.zeros((LANES - N_HEADS, d), F32)]).astype(BF16)
    w_gate = jnp.concatenate([seg(4), seg(8), seg(10), seg(11), seg(12), seg(13)]).astype(BF16)
    tr = lambda a: jnp.swapaxes(a, 1, 2)
    return dict(
        g_row=norm_g[l][None, :], w_qkv=w_qkv, w_ff=w_ff, w_gate=w_gate,
        bf_col=b_fgate[l][:, None], bf_row=b_fgate[l][None, :],
        qn_col=qn_g[l][:, None], kn_col=kn_g[l][:, None],
        qn_row=jnp.tile(qn_g[l], N_HEADS)[None, :], kn_row=jnp.tile(kn_g[l], N_HEADS)[None, :],
        a_re=ssm_a_re[l].reshape(1, N_STATE), a_im=ssm_a_im[l].reshape(1, N_STATE),
        ldt=jnp.repeat(ssm_log_dt[l], SSM_STATE)[None, :],
        bre=_block_diag(tr(ssm_b_re[l])).astype(BF16),
        bim=_block_diag(tr(ssm_b_im[l])).astype(BF16),
        cre=_block_diag(tr(ssm_c_re[l])).astype(BF16),
        cim=_block_diag(tr(ssm_c_im[l])).astype(BF16),
        d_row=ssm_d[l][None, :], wglu=w_glu[l].astype(BF16),
        w_br=jnp.stack([w_br_fox[l], w_br_sb[l], w_br_ssm[l]]).astype(BF16),
        w_out=w_out[l].astype(BF16))


def kernel(x_prompt, x_sample, cache_fox_k, cache_fox_v, cache_fox_logf, cache_sb_k, cache_sb_v, state_ssm_re, state_ssm_im, page_table, norm_g, w_in, b_fgate, qn_g, kn_g, ssm_a_re, ssm_a_im, ssm_log_dt, ssm_b_re, ssm_b_im, ssm_c_re, ssm_c_im, ssm_d, w_glu, w_br_fox, w_br_sb, w_br_ssm, w_out):
    depth = w_in.shape[0]
    nb, t, d = x_prompt.shape
    ns = x_sample.shape[0]
    to_pages = lambda c: jnp.transpose(c, (0, 1, 3, 4, 2))
    fkc, fvc, skc, svc = (to_pages(c) for c in (cache_fox_k, cache_fox_v, cache_sb_k, cache_sb_v))
    lfc = jnp.transpose(cache_fox_logf, (0, 1, 3, 2))
    headmean = _block_diag(jnp.full((N_HEADS, HEAD_DIM, HEAD_DIM), 1.0 / HEAD_DIM, F32))
    col4 = lambda a: a.reshape(ns, N_HEADS, HEAD_DIM, 1)

    yp = x_prompt
    ys = x_sample.reshape(ns, d)
    ents_p, ents_s = [], []
    for l in range(depth):
        p = _prep_layer(l, norm_g, w_in, b_fgate, qn_g, kn_g, ssm_a_re, ssm_a_im, ssm_log_dt,
                        ssm_b_re, ssm_b_im, ssm_c_re, ssm_c_im, ssm_d, w_glu, w_br_fox,
                        w_br_sb, w_br_ssm, w_out)
        fq, fk, fv, lf, lfn, sq, sk, sv, u = _inproj_prompt(
            yp, p["g_row"], p["w_qkv"], p["w_ff"], p["bf_col"], p["bf_row"], p["qn_col"],
            p["kn_col"], tm=512)
        o_fox = _fox_prompt(fq, fk, fv, lf, lfn, tq=256)
        o_sb = _sb_prompt(sq, sk, sv, tq=256, tk=128)
        u_tm = jnp.swapaxes(u, 0, 1).reshape(t * nb, W_SSM)
        o_tm, hre, him = _ssm_prompt(u_tm, nb, p["a_re"], p["a_im"], p["ldt"], p["bre"],
                                     p["bim"], p["cre"], p["cim"], p["d_row"], p["wglu"], tt=128)
        o_ssm = jnp.swapaxes(o_tm.reshape(t, nb, W_SSM), 0, 1)
        yp = _merge(yp, o_fox, o_sb, o_ssm, p["g_row"], p["w_gate"], p["w_br"], p["w_out"],
                    tm=512, attn_transposed=True)
        ents_p.append((fk, fv, lf, sk, sv, hre, him))
        (dfq, dfk, dfv, dlf, dsq, dsk, dsv, dom, h1r, h1i) = _decode_pre(
            ys, p["g_row"], p["w_qkv"], p["w_ff"], p["bf_row"], p["qn_row"], p["kn_row"],
            headmean, p["a_re"], p["a_im"], p["ldt"], p["bre"], p["bim"], p["cre"], p["cim"],
            p["d_row"], p["wglu"], state_ssm_re[l].reshape(ns, N_STATE),
            state_ssm_im[l].reshape(ns, N_STATE))
        do_fox = _fox_decode(page_table, col4(dfq), col4(dfk), col4(dfv), dlf[:, :, None],
                             fkc, fvc, lfc, l, g=4).reshape(ns, W_ATT)
        do_sb = _sb_decode(page_table, col4(dsq), skc, svc, l, g=4).reshape(ns, W_ATT)
        ys = _merge(ys[None], do_fox[None], do_sb[None], dom[None], p["g_row"], p["w_gate"],
                    p["w_br"], p["w_out"], tm=ns, attn_transposed=False)[0]
        ents_s.append((dfk, dfv, dlf, dsk, dsv, h1r, h1i))

    def prompt_kv(i):
        a = jnp.stack([e[i] for e in ents_p]).reshape(depth, nb, N_HEADS, HEAD_DIM, t)
        return jnp.transpose(a, (0, 1, 4, 2, 3))

    p_logf = jnp.transpose(jnp.stack([e[2] for e in ents_p]), (0, 1, 3, 2))
    p_re = jnp.stack([e[5] for e in ents_p]).reshape(depth, nb, SSM_GROUPS, SSM_STATE)
    p_im = jnp.stack([e[6] for e in ents_p]).reshape(depth, nb, SSM_GROUPS, SSM_STATE)
    s_kv = lambda i: jnp.stack([e[i] for e in ents_s]).reshape(depth, ns, 1, N_HEADS, HEAD_DIM)
    s_logf = jnp.stack([e[2] for e in ents_s]).reshape(depth, ns, 1, N_HEADS)
    s_re = jnp.stack([e[5] for e in ents_s]).reshape(depth, ns, SSM_GROUPS, SSM_STATE)
    s_im = jnp.stack([e[6] for e in ents_s]).reshape(depth, ns, SSM_GROUPS, SSM_STATE)
    return (yp, ys.reshape(ns, 1, d), prompt_kv(0), prompt_kv(1), p_logf, prompt_kv(3),
            prompt_kv(4), p_re, p_im, s_kv(0), s_kv(1), s_logf, s_kv(3), s_kv(4), s_re, s_im)
```

```python
import functools
import math

import jax
import jax.numpy as jnp
from jax import lax
from jax.experimental import pallas as pl
from jax.experimental.pallas import tpu as pltpu

HEAD_DIM = 64
N_HEADS = 8
W_ATT = N_HEADS * HEAD_DIM
SSM_GROUP = 16
SSM_GROUPS = 32
SSM_STATE = 64
W_SSM = SSM_GROUP * SSM_GROUPS
N_STATE = SSM_GROUPS * SSM_STATE
EPS = 1e-6
NEG_INF = -1e30
SCALE = HEAD_DIM ** -0.5
SB_SKIP_LOG = -110.0

LANES = 128
V7X_VMEM_BYTES = 64 * 1024 * 1024
VMEM_LIMIT = V7X_VMEM_BYTES - 8 * 1024 * 1024

F32 = jnp.float32
BF16 = jnp.bfloat16


def _cparams(*sem):
    return pltpu.CompilerParams(dimension_semantics=sem, vmem_limit_bytes=VMEM_LIMIT)


def _dot_nn(a, b, precision=None):
    return lax.dot_general(a, b, (((1,), (0,)), ((), ())), precision=precision,
                           preferred_element_type=F32)


def _dot_nt(a, b):
    return lax.dot_general(a, b, (((1,), (1,)), ((), ())), preferred_element_type=F32)


def _dot_tn(a, b):
    return lax.dot_general(a, b, (((0,), (0,)), ((), ())), preferred_element_type=F32)


def _log_sigmoid(x):
    return jnp.minimum(x, 0.0) - jnp.log1p(jnp.exp(-jnp.abs(x)))


def _softplus(x):
    return jnp.maximum(x, 0.0) + jnp.log1p(jnp.exp(-jnp.abs(x)))


def _sigmoid(x):
    return jax.nn.sigmoid(x)


def _gelu_tanh(x):
    c = math.sqrt(2.0 / math.pi)
    return x * (0.5 * (1.0 + jnp.tanh(c * (x + 0.044715 * (x * x * x)))))


def _rms_rows(x, g):
    ms = jnp.mean(x * x, axis=-1, keepdims=True)
    return x * lax.rsqrt(ms + EPS) * g


def _headnorm_t(xt, g_col):
    n = xt.shape[-1]
    x3 = xt.reshape(N_HEADS, HEAD_DIM, n)
    ms = jnp.mean(x3 * x3, axis=1, keepdims=True)
    y = x3 * lax.rsqrt(ms + EPS) * g_col[None]
    return y.reshape(W_ATT, n)


def _inproj_prompt_kernel(x_ref, g_ref, w_ref, wff_ref, bfc_ref, bfr_ref, qn_ref, kn_ref,
                          fq_ref, fk_ref, fv_ref, lf_ref, lfn_ref, sq_ref, sk_ref, sv_ref,
                          u_ref):
    h = _rms_rows(x_ref[...], g_ref[...]).astype(BF16)

    def proj_t(i):
        return _dot_nt(w_ref[i * W_ATT:(i + 1) * W_ATT, :], h)

    fq_ref[...] = (_headnorm_t(proj_t(0), qn_ref[...]) * SCALE).astype(BF16)
    fk_ref[...] = _headnorm_t(proj_t(1), kn_ref[...])
    fv_ref[...] = proj_t(2)
    wff = wff_ref[...]
    lf_ref[...] = _log_sigmoid(_dot_nt(wff, h)[:N_HEADS, :] + bfc_ref[...])
    lfn_ref[...] = _log_sigmoid(_dot_nt(h, wff)[:, :N_HEADS] + bfr_ref[...])
    sq_ref[...] = (proj_t(3) * SCALE).astype(BF16)
    sk_ref[...] = proj_t(4)
    sv_ref[...] = proj_t(5)
    u_ref[...] = _dot_nt(h, w_ref[6 * W_ATT:7 * W_ATT, :])


def _inproj_prompt(x, g_row, w_qkv, w_ff, bf_col, bf_row, qn_col, kn_col, tm):
    b, t, d = x.shape
    tm = min(tm, t)
    full = lambda a: pl.BlockSpec(a.shape, lambda i, j: (0,) * a.ndim)
    tspec = pl.BlockSpec((None, W_ATT, tm), lambda i, j: (i, 0, j))
    kv = jax.ShapeDtypeStruct((b, W_ATT, t), F32)
    qs = jax.ShapeDtypeStruct((b, W_ATT, t), BF16)
    return pl.pallas_call(
        _inproj_prompt_kernel,
        grid=(b, t // tm),
        in_specs=[pl.BlockSpec((None, tm, d), lambda i, j: (i, j, 0)), full(g_row), full(w_qkv),
                  full(w_ff), full(bf_col), full(bf_row), full(qn_col), full(kn_col)],
        out_specs=[tspec, tspec, tspec,
                   pl.BlockSpec((None, N_HEADS, tm), lambda i, j: (i, 0, j)),
                   pl.BlockSpec((None, tm, N_HEADS), lambda i, j: (i, j, 0)),
                   tspec, tspec, tspec,
                   pl.BlockSpec((None, tm, W_SSM), lambda i, j: (i, j, 0))],
        out_shape=[qs, kv, kv, jax.ShapeDtypeStruct((b, N_HEADS, t), F32),
                   jax.ShapeDtypeStruct((b, t, N_HEADS), F32), qs, kv, kv,
                   jax.ShapeDtypeStruct((b, t, W_SSM), F32)],
        compiler_params=_cparams("parallel", "parallel"),
        name="inproj_prompt",
    )(x, g_row, w_qkv, w_ff, bf_col, bf_row, qn_col, kn_col)


def _fox_prompt_kernel(q_ref, k_ref, v_ref, lf_ref, lfn_ref, o_ref,
                       kb_ref, vb_ref, crow_ref, cb_ref, acc_ref, m_ref, l_ref, *, tq, cb):
    qi = pl.program_id(1)
    t = k_ref.shape[-1]

    @pl.when(qi == 0)
    def _():
        kb_ref[...] = k_ref[...].astype(BF16)
        vb_ref[...] = v_ref[...].astype(BF16)
        r = lax.broadcasted_iota(jnp.int32, (cb, cb), 0)
        c = lax.broadcasted_iota(jnp.int32, (cb, cb), 1)
        upper = (c > r).astype(F32)
        lower = (r > c).astype(F32)
        carry_c = jnp.zeros((1, N_HEADS), F32)
        carry_r = jnp.zeros((N_HEADS, 1), F32)
        for blk in reversed(range(t // cb)):
            seg_c = lfn_ref[blk * cb:(blk + 1) * cb, :]
            within_c = _dot_nn(upper, seg_c, precision=lax.Precision.HIGHEST)
            c_col = -(within_c + carry_c)
            for h in range(N_HEADS):
                cb_ref[h, blk * cb:(blk + 1) * cb, :] = jnp.broadcast_to(c_col[:, h:h + 1],
                                                                         (cb, LANES))
            carry_c = carry_c + jnp.sum(seg_c, axis=0, keepdims=True)
            seg_r = lf_ref[:, blk * cb:(blk + 1) * cb]
            within_r = _dot_nn(seg_r, lower, precision=lax.Precision.HIGHEST)
            crow_ref[:, blk * cb:(blk + 1) * cb] = -(within_r + carry_r)
            carry_r = carry_r + jnp.sum(seg_r, axis=1, keepdims=True)

    m_ref[...] = jnp.full(m_ref.shape, NEG_INF, F32)
    l_ref[...] = jnp.zeros_like(l_ref)
    acc_ref[...] = jnp.zeros_like(acc_ref)
    q0 = pl.multiple_of(qi * tq, tq)

    def tile(j, masked):
        k0 = pl.multiple_of(j * tq, tq)
        for h in range(N_HEADS):
            rows = slice(h * HEAD_DIM, (h + 1) * HEAD_DIM)
            s = _dot_tn(kb_ref[rows, pl.ds(k0, tq)], q_ref[rows, :])
            ck = cb_ref[h, pl.ds(k0, tq), :]
            ck = jnp.concatenate([ck] * (tq // LANES), axis=1)
            s = s + (crow_ref[h:h + 1, pl.ds(q0, tq)] - ck)
            if masked:
                kpos = lax.broadcasted_iota(jnp.int32, (tq, tq), 0)
                qpos = lax.broadcasted_iota(jnp.int32, (tq, tq), 1)
                s = jnp.where(kpos <= qpos, s, NEG_INF)
            m_old = m_ref[h:h + 1, :]
            m_new = jnp.maximum(m_old, jnp.max(s, axis=0, keepdims=True))
            p = jnp.exp(s - m_new)
            alpha = jnp.exp(m_old - m_new)
            l_ref[h:h + 1, :] = alpha * l_ref[h:h + 1, :] + jnp.sum(p, axis=0, keepdims=True)
            m_ref[h:h + 1, :] = m_new
            acc_ref[rows, :] = alpha * acc_ref[rows, :] + _dot_nn(vb_ref[rows, pl.ds(k0, tq)],
                                                                  p.astype(BF16))

    def body(j, carry):
        tile(j, False)
        return carry

    lax.fori_loop(0, qi, body, 0)
    tile(qi, True)
    for h in range(N_HEADS):
        rows = slice(h * HEAD_DIM, (h + 1) * HEAD_DIM)
        o_ref[rows, :] = acc_ref[rows, :] / l_ref[h:h + 1, :]


def _fox_prompt(q_t, k_t, v_t, lf_t, lf_n, tq):
    b, _, t = k_t.shape
    tq = min(tq, t)
    whole = pl.BlockSpec((None, W_ATT, t), lambda i, j: (i, 0, 0))
    return pl.pallas_call(
        functools.partial(_fox_prompt_kernel, tq=tq, cb=min(256, t)),
        grid=(b, t // tq),
        in_specs=[pl.BlockSpec((None, W_ATT, tq), lambda i, j: (i, 0, j)), whole, whole,
                  pl.BlockSpec((None, N_HEADS, t), lambda i, j: (i, 0, 0)),
                  pl.BlockSpec((None, t, N_HEADS), lambda i, j: (i, 0, 0))],
        out_specs=pl.BlockSpec((None, W_ATT, tq), lambda i, j: (i, 0, j)),
        out_shape=jax.ShapeDtypeStruct((b, W_ATT, t), F32),
        scratch_shapes=[pltpu.VMEM((W_ATT, t), BF16), pltpu.VMEM((W_ATT, t), BF16),
                        pltpu.VMEM((N_HEADS, t), F32), pltpu.VMEM((N_HEADS, t, LANES), F32),
                        pltpu.VMEM((W_ATT, tq), F32), pltpu.VMEM((N_HEADS, tq), F32),
                        pltpu.VMEM((N_HEADS, tq), F32)],
        compiler_params=_cparams("parallel", "arbitrary"),
        name="fox_prompt",
    )(q_t, k_t, v_t, lf_t, lf_n)


def _sb_prompt_kernel(q_ref, k_ref, v_ref, o_ref, kb_ref, vb_ref, acc_ref, run_ref,
                      *, tq, tk, skip_log):
    qi = pl.program_id(1)
    q0 = qi * tq
    nk = tq // tk

    @pl.when(qi == 0)
    def _():
        kb_ref[...] = k_ref[...].astype(BF16)
        vb_ref[...] = v_ref[...].astype(BF16)

    acc_ref[...] = jnp.zeros_like(acc_ref)
    run_ref[...] = jnp.zeros_like(run_ref)
    r = lax.broadcasted_iota(jnp.int32, (tk, tk), 0)
    c = lax.broadcasted_iota(jnp.int32, (tk, tk), 1)
    upper = (c > r).astype(BF16)

    def tile(kj, masked):
        k0 = pl.multiple_of(kj * tk, tk)
        for h in range(N_HEADS):
            rows = slice(h * HEAD_DIM, (h + 1) * HEAD_DIM)
            z = _dot_tn(kb_ref[rows, pl.ds(k0, tk)], q_ref[rows, :])
            sp = _softplus(z)
            if masked:
                kpos = k0 + lax.broadcasted_iota(jnp.int32, (tk, tq), 0)
                qpos = q0 + lax.broadcasted_iota(jnp.int32, (tk, tq), 1)
                mask = kpos < qpos
                lk = jnp.where(mask, -sp, 0.0)
            else:
                lk = -sp
            lk_hi = lk.astype(BF16)
            lk_lo = (lk - lk_hi.astype(F32)).astype(BF16)
            within = _dot_nn(upper, lk_hi) + _dot_nn(upper, lk_lo)
            run = run_ref[h:h + 1, :]
            w = jnp.exp((z - sp) + (within + run))
            if masked:
                w = jnp.where(mask, w, 0.0)
            acc_ref[rows, :] += _dot_nn(vb_ref[rows, pl.ds(k0, tk)], w.astype(BF16))
            run_ref[h:h + 1, :] = run + jnp.sum(lk, axis=0, keepdims=True)

    for j in range(nk):
        tile(qi * nk + (nk - 1) - j, True)

    def live():
        return (jnp.max(run_ref[...]) > skip_log).astype(jnp.int32)

    def cond(carry):
        kj, alive = carry
        return jnp.logical_and(kj >= 0, alive > 0)

    def body(carry):
        kj, _ = carry
        tile(kj, False)
        return kj - 1, live()

    lax.while_loop(cond, body, (qi * nk - 1, live()))
    o_ref[...] = acc_ref[...]


def _sb_prompt(q_t, k_t, v_t, tq, tk):
    b, _, t = k_t.shape
    tq = min(tq, t)
    tk = min(tk, tq)
    whole = pl.BlockSpec((None, W_ATT, t), lambda i, j: (i, 0, 0))
    return pl.pallas_call(
        functools.partial(_sb_prompt_kernel, tq=tq, tk=tk, skip_log=SB_SKIP_LOG),
        grid=(b, t // tq),
        in_specs=[pl.BlockSpec((None, W_ATT, tq), lambda i, j: (i, 0, j)), whole, whole],
        out_specs=pl.BlockSpec((None, W_ATT, tq), lambda i, j: (i, 0, j)),
        out_shape=jax.ShapeDtypeStruct((b, W_ATT, t), F32),
        scratch_shapes=[pltpu.VMEM((W_ATT, t), BF16), pltpu.VMEM((W_ATT, t), BF16),
                        pltpu.VMEM((W_ATT, tq), F32), pltpu.VMEM((N_HEADS, tq), F32)],
        compiler_params=_cparams("parallel", "arbitrary"),
        name="sb_prompt",
    )(q_t, k_t, v_t)


def _ssm_discretise(a_re, a_im, log_dt):
    dt = jnp.exp(log_dt)
    mag = jnp.exp(a_re * dt)
    ang = a_im * dt
    lb_re = mag * jnp.cos(ang)
    lb_im = mag * jnp.sin(ang)
    den = a_re * a_re + a_im * a_im
    nr = lb_re - 1.0
    coef_re = (nr * a_re + lb_im * a_im) / den
    coef_im = (lb_im * a_re - nr * a_im) / den
    return lb_re, lb_im, coef_re, coef_im


def _ssm_readout(h_re, h_im, u, cre_ref, cim_ref, d_row, wglu_ref):
    half_s, half_w = N_STATE // 2, W_SSM // 2
    hr = h_re.astype(BF16)
    hi = h_im.astype(BF16)
    y_lo = (_dot_nn(hr[:, :half_s], cre_ref[:half_s, :half_w])
            - _dot_nn(hi[:, :half_s], cim_ref[:half_s, :half_w]))
    y_hi = (_dot_nn(hr[:, half_s:], cre_ref[half_s:, half_w:])
            - _dot_nn(hi[:, half_s:], cim_ref[half_s:, half_w:]))
    y = jnp.concatenate([y_lo, y_hi], axis=1) + d_row * u
    y = _gelu_tanh(y)
    return y * _sigmoid(_dot_nn(y.astype(BF16), wglu_ref[...]))


def _ssm_prompt_kernel(u_ref, are_ref, aim_ref, ldt_ref, bre_ref, bim_ref, cre_ref, cim_ref,
                       d_ref, wglu_ref, o_ref, hre_ref, him_ref,
                       xre_ref, xim_ref, par_ref, hst_ref, *, nb, tt, lane_chunk):
    i = pl.program_id(0)

    @pl.when(i == 0)
    def _():
        pars = _ssm_discretise(are_ref[...], aim_ref[...], ldt_ref[...])
        for k in range(4):
            par_ref[k] = jnp.broadcast_to(pars[k], (nb, N_STATE))
        hst_ref[...] = jnp.zeros_like(hst_ref)

    ub = u_ref[...].astype(BF16)
    half_s, half_w = N_STATE // 2, W_SSM // 2
    xre_ref[:, :half_s] = _dot_nn(ub[:, :half_w], bre_ref[:half_w, :half_s])
    xre_ref[:, half_s:] = _dot_nn(ub[:, half_w:], bre_ref[half_w:, half_s:])
    xim_ref[:, :half_s] = _dot_nn(ub[:, :half_w], bim_ref[:half_w, :half_s])
    xim_ref[:, half_s:] = _dot_nn(ub[:, half_w:], bim_ref[half_w:, half_s:])

    for c in range(N_STATE // lane_chunk):
        cols = slice(c * lane_chunk, (c + 1) * lane_chunk)
        lb_re, lb_im = par_ref[0, :, cols], par_ref[1, :, cols]
        cf_re, cf_im = par_ref[2, :, cols], par_ref[3, :, cols]

        def body(t, carry):
            h_re, h_im = carry
            rws = pl.ds(pl.multiple_of(t * nb, nb), nb)
            b_re, b_im = xre_ref[rws, cols], xim_ref[rws, cols]
            x_re = cf_re * b_re - cf_im * b_im
            x_im = cf_re * b_im + cf_im * b_re
            n_re = lb_re * h_re - lb_im * h_im + x_re
            n_im = lb_re * h_im + lb_im * h_re + x_im
            xre_ref[rws, cols] = n_re
            xim_ref[rws, cols] = n_im
            return n_re, n_im

        h_re, h_im = lax.fori_loop(0, tt, body, (hst_ref[0, :, cols], hst_ref[1, :, cols]),
                                   unroll=8)
        hst_ref[0, :, cols] = h_re
        hst_ref[1, :, cols] = h_im

    o_ref[...] = _ssm_readout(xre_ref[...], xim_ref[...], u_ref[...], cre_ref, cim_ref,
                              d_ref[...], wglu_ref)
    hre_ref[...] = hst_ref[0]
    him_ref[...] = hst_ref[1]


def _ssm_prompt(u_tm, nb, a_re, a_im, ldt, bre, bim, cre, cim, d_row, wglu, tt):
    t = u_tm.shape[0] // nb
    tt = min(tt, t)
    full = lambda a: pl.BlockSpec(a.shape, lambda i: (0,) * a.ndim)
    st = jax.ShapeDtypeStruct((nb, N_STATE), F32)
    return pl.pallas_call(
        functools.partial(_ssm_prompt_kernel, nb=nb, tt=tt, lane_chunk=512),
        grid=(t // tt,),
        in_specs=[pl.BlockSpec((nb * tt, W_SSM), lambda i: (i, 0)), full(a_re), full(a_im),
                  full(ldt), full(bre), full(bim), full(cre), full(cim), full(d_row), full(wglu)],
        out_specs=[pl.BlockSpec((nb * tt, W_SSM), lambda i: (i, 0)),
                   pl.BlockSpec((nb, N_STATE), lambda i: (0, 0)),
                   pl.BlockSpec((nb, N_STATE), lambda i: (0, 0))],
        out_shape=[jax.ShapeDtypeStruct((t * nb, W_SSM), F32), st, st],
        scratch_shapes=[pltpu.VMEM((nb * tt, N_STATE), F32), pltpu.VMEM((nb * tt, N_STATE), F32),
                        pltpu.VMEM((4, nb, N_STATE), F32), pltpu.VMEM((2, nb, N_STATE), F32)],
        compiler_params=_cparams("arbitrary"),
        name="ssm_prompt",
    )(u_tm, a_re, a_im, ldt, bre, bim, cre, cim, d_row, wglu)


def _merge_kernel(x_ref, of_ref, os_ref, om_ref, g_ref, wg_ref, wbr_ref, wo_ref, y_ref,
                  *, attn_transposed):
    x = x_ref[...]
    h = _rms_rows(x, g_ref[...]).astype(BF16)

    def proj(lo, n):
        return _dot_nt(h, wg_ref[lo:lo + n, :])

    def branch(o, z_lo, wi, g_lo):
        z = proj(z_lo, W_ATT)
        a = (o * (z * _sigmoid(z))).astype(BF16)
        br = _dot_nn(a, wbr_ref[wi])
        return _sigmoid(proj(g_lo, x.shape[-1])) * br

    of = of_ref[...].T if attn_transposed else of_ref[...]
    os_ = os_ref[...].T if attn_transposed else os_ref[...]
    d = x.shape[-1]
    m = (branch(of, 0, 0, 3 * W_ATT) + branch(os_, W_ATT, 1, 3 * W_ATT + d)
         + branch(om_ref[...], 2 * W_ATT, 2, 3 * W_ATT + 2 * d))
    y_ref[...] = x + _dot_nn(m.astype(BF16), wo_ref[...])


def _merge(x, of, os_, om, g_row, w_gate, w_br, w_out, tm, attn_transposed):
    b, t, d = x.shape
    tm = min(tm, t)
    full = lambda a: pl.BlockSpec(a.shape, lambda i, j: (0,) * a.ndim)
    rspec = lambda n: pl.BlockSpec((None, tm, n), lambda i, j: (i, j, 0))
    aspec = pl.BlockSpec((None, W_ATT, tm), lambda i, j: (i, 0, j)) if attn_transposed \
        else rspec(W_ATT)
    return pl.pallas_call(
        functools.partial(_merge_kernel, attn_transposed=attn_transposed),
        grid=(b, t // tm),
        in_specs=[rspec(d), aspec, aspec, rspec(W_SSM), full(g_row), full(w_gate), full(w_br),
                  full(w_out)],
        out_specs=rspec(d),
        out_shape=jax.ShapeDtypeStruct((b, t, d), F32),
        compiler_params=_cparams("parallel", "parallel"),
        name="merge_t" if attn_transposed else "merge_n",
    )(x, of, os_, om, g_row, w_gate, w_br, w_out)


def _decode_pre_kernel(x_ref, g_ref, w_ref, wff_ref, bfr_ref, qn_ref, kn_ref, hd_ref,
                       are_ref, aim_ref, ldt_ref, bre_ref, bim_ref, cre_ref, cim_ref, d_ref,
                       wglu_ref, h0r_ref, h0i_ref,
                       fq_ref, fk_ref, fv_ref, lf_ref, sq_ref, sk_ref, sv_ref, om_ref,
                       h1r_ref, h1i_ref):
    h = _rms_rows(x_ref[...], g_ref[...]).astype(BF16)

    def proj(i):
        return _dot_nt(h, w_ref[i * W_ATT:(i + 1) * W_ATT, :])

    def headnorm(x, g_row):
        ms = _dot_nn(x * x, hd_ref[...], precision=lax.Precision.HIGHEST)
        return x * lax.rsqrt(ms + EPS) * g_row

    fq_ref[...] = headnorm(proj(0), qn_ref[...]) * SCALE
    fk_ref[...] = headnorm(proj(1), kn_ref[...])
    fv_ref[...] = proj(2)
    lf_ref[...] = _log_sigmoid(_dot_nt(h, wff_ref[...])[:, :N_HEADS] + bfr_ref[...])
    sq_ref[...] = proj(3) * SCALE
    sk_ref[...] = proj(4)
    sv_ref[...] = proj(5)
    u = proj(6)
    lb_re, lb_im, cf_re, cf_im = _ssm_discretise(are_ref[...], aim_ref[...], ldt_ref[...])
    ub = u.astype(BF16)
    b_re = _dot_nn(ub, bre_ref[...])
    b_im = _dot_nn(ub, bim_ref[...])
    x_re = cf_re * b_re - cf_im * b_im
    x_im = cf_re * b_im + cf_im * b_re
    r0, i0 = h0r_ref[...], h0i_ref[...]
    h_re = x_re + lb_re * r0 - lb_im * i0
    h_im = x_im + lb_re * i0 + lb_im * r0
    h1r_ref[...] = h_re
    h1i_ref[...] = h_im
    om_ref[...] = _ssm_readout(h_re, h_im, u, cre_ref, cim_ref, d_ref[...], wglu_ref)


def _decode_pre(x, g_row, w_qkv, w_ff, bf_row, qn_row, kn_row, headmean, a_re, a_im, ldt,
                bre, bim, cre, cim, d_row, wglu, h0r, h0i):
    n = x.shape[0]
    att = jax.ShapeDtypeStruct((n, W_ATT), F32)
    st = jax.ShapeDtypeStruct((n, N_STATE), F32)
    return pl.pallas_call(
        _decode_pre_kernel,
        out_shape=[att, att, att, jax.ShapeDtypeStruct((n, N_HEADS), F32), att, att, att,
                   jax.ShapeDtypeStruct((n, W_SSM), F32), st, st],
        compiler_params=pltpu.CompilerParams(vmem_limit_bytes=VMEM_LIMIT),
        name="decode_pre",
    )(x, g_row, w_qkv, w_ff, bf_row, qn_row, kn_row, headmean, a_re, a_im, ldt, bre, bim,
      cre, cim, d_row, wglu, h0r, h0i)


def _suffix_sums(blocks):
    x = jnp.concatenate(blocks, axis=0)
    r = lax.broadcasted_iota(jnp.int32, (LANES, LANES), 0)
    c = lax.broadcasted_iota(jnp.int32, (LANES, LANES), 1)
    y = _dot_nn(x, (r > c).astype(F32), precision=lax.Precision.HIGHEST)
    return [y[i * N_HEADS:(i + 1) * N_HEADS] for i in range(len(blocks))]


def _head_block_rows(row):
    head = lax.broadcasted_iota(jnp.int32, (N_HEADS, W_ATT), 0)
    lane = lax.broadcasted_iota(jnp.int32, (N_HEADS, W_ATT), 1)
    return jnp.where(lane // HEAD_DIM == head, jnp.broadcast_to(row, (N_HEADS, W_ATT)), 0.0)


def _page_scores(qblk, k_ref):
    return _dot_nn(qblk.astype(BF16), k_ref[...].reshape(W_ATT, k_ref.shape[-1]).astype(BF16))


def _weights_times_pages(weights, v_refs):
    page = v_refs[0].shape[-1]
    v_all = jnp.concatenate([v[...].reshape(W_ATT, page).astype(BF16) for v in v_refs], axis=1)
    w_all = jnp.concatenate(weights, axis=1).astype(BF16)
    return _dot_nt(w_all, v_all)


def _head_rows_to_row(x):
    head = lax.broadcasted_iota(jnp.int32, x.shape, 0)
    lane = lax.broadcasted_iota(jnp.int32, x.shape, 1)
    return jnp.sum(jnp.where(lane // HEAD_DIM == head, x, 0.0), axis=0, keepdims=True)


def _per_head_to_row(col):
    return _head_rows_to_row(jnp.broadcast_to(col, (N_HEADS, W_ATT)))


def _fox_decode_kernel(pt_ref, q_ref, kn_ref, vn_ref, lfn_ref, *refs, g, steps):
    k_refs, v_refs, lf_refs = refs[:g], refs[g:2 * g], refs[2 * g:3 * g]
    o_ref, qblk_ref, s_ref, acc_ref, m_ref, l_ref, run_ref = refs[3 * g:]
    j = pl.program_id(1)
    page = k_refs[0].shape[-1]

    @pl.when(j == 0)
    def _():
        qblk = _head_block_rows(q_ref[...])
        qblk_ref[...] = qblk
        m_ref[...] = jnp.sum(qblk * kn_ref[...], axis=1, keepdims=True)
        run_ref[...] = lfn_ref[...]
        acc_ref[...] = jnp.zeros_like(acc_ref)

    @pl.when(j < steps)
    def _():
        lfs = [lf_refs[i][...] for i in range(g)]
        suffix = _suffix_sums(lfs)
        run = run_ref[...]
        m_new = m_ref[...]
        for i in range(g):
            s = _page_scores(qblk_ref[...], k_refs[i]) + (run + suffix[i])
            run = run + jnp.sum(lfs[i], axis=1, keepdims=True)
            m_new = jnp.maximum(m_new, jnp.max(s, axis=1, keepdims=True))
            s_ref[:, pl.ds(pl.multiple_of((j * g + i) * page, page), page)] = s
        run_ref[...] = run
        m_ref[...] = m_new

    @pl.when(j == steps)
    def _():
        l_ref[...] = jnp.zeros_like(l_ref)

    @pl.when(j >= steps)
    def _():
        m = m_ref[...]
        probs = []
        l_new = l_ref[...]
        for i in range(g):
            off = pl.multiple_of(((j - steps) * g + i) * page, page)
            p = jnp.exp(s_ref[:, pl.ds(off, page)] - m)
            l_new = l_new + jnp.sum(p, axis=1, keepdims=True)
            probs.append(p)
        l_ref[...] = l_new
        acc_ref[...] += _weights_times_pages(probs, v_refs)

    @pl.when(j == 2 * steps - 1)
    def _():
        w_self = jnp.exp(jnp.sum(qblk_ref[...] * kn_ref[...], axis=1, keepdims=True)
                         - m_ref[...])
        o = _head_rows_to_row(acc_ref[...]) + _per_head_to_row(w_self) * vn_ref[...]
        o_ref[...] = o / _per_head_to_row(l_ref[...] + w_self)


def _sb_decode_kernel(pt_ref, q_ref, run0_ref, *refs, g):
    k_refs, v_refs = refs[:g], refs[g:2 * g]
    o_ref, runo_ref, qblk_ref, acc_ref, run_ref = refs[2 * g:]
    j = pl.program_id(1)

    @pl.when(j == 0)
    def _():
        qblk_ref[...] = _head_block_rows(q_ref[...])
        acc_ref[...] = jnp.zeros_like(acc_ref)
        run_ref[...] = run0_ref[...]

    zs = [_page_scores(qblk_ref[...], k_refs[i]) for i in range(g)]
    sps = [_softplus(z) for z in zs]
    suffix = _suffix_sums([-sp for sp in sps])
    run = run_ref[...]
    weights = []
    for i in range(g):
        weights.append(jnp.exp((zs[i] - sps[i]) + (suffix[i] + run)))
        run = run - jnp.sum(sps[i], axis=1, keepdims=True)
    run_ref[...] = run
    acc_ref[...] += _weights_times_pages(weights, v_refs)

    @pl.when(j == pl.num_programs(1) - 1)
    def _():
        o_ref[...] = _head_rows_to_row(acc_ref[...])
        runo_ref[...] = run_ref[...]


def _page_spec(layer, last_page, g, i, tail, step_of=lambda j: j):
    def index(b, j, pt):
        return (layer, pt[b, last_page - (step_of(j) * g + i)]) + (0,) * len(tail)
    return pl.BlockSpec((None, None) + tail, index)


def _fox_decode(page_table, q_row, k_new_row, v_new_row, lf_new, kc, vc, lfc, layer, g):
    n, n_pages = page_table.shape
    page = kc.shape[-1]
    g = min(g, n_pages)
    steps = n_pages // g
    row = pl.BlockSpec((None, 1, W_ATT), lambda b, j, pt: (b, 0, 0))
    hcol = pl.BlockSpec((None, N_HEADS, 1), lambda b, j, pt: (b, 0, 0))
    kv_tail = (N_HEADS, HEAD_DIM, page)
    last = n_pages - 1
    pass1 = lambda j: jnp.minimum(j, steps - 1)
    pass2 = lambda j: jnp.maximum(j - steps, 0)
    in_specs = ([row, row, row, hcol]
                + [_page_spec(layer, last, g, i, kv_tail, pass1) for i in range(g)]
                + [_page_spec(layer, last, g, i, kv_tail, pass2) for i in range(g)]
                + [_page_spec(layer, last, g, i, (N_HEADS, page), pass1) for i in range(g)])
    return pl.pallas_call(
        functools.partial(_fox_decode_kernel, g=g, steps=steps),
        grid_spec=pltpu.PrefetchScalarGridSpec(
            num_scalar_prefetch=1, grid=(n, 2 * steps), in_specs=in_specs, out_specs=row,
            scratch_shapes=[pltpu.VMEM((N_HEADS, W_ATT), F32),
                            pltpu.VMEM((N_HEADS, n_pages * page), F32),
                            pltpu.VMEM((N_HEADS, W_ATT), F32),
                            pltpu.VMEM((N_HEADS, 1), F32), pltpu.VMEM((N_HEADS, 1), F32),
                            pltpu.VMEM((N_HEADS, 1), F32)]),
        out_shape=jax.ShapeDtypeStruct((n, 1, W_ATT), F32),
        compiler_params=_cparams("parallel", "arbitrary"),
        name="fox_decode",
    )(page_table, q_row, k_new_row, v_new_row, lf_new, *([kc] * g), *([vc] * g), *([lfc] * g))


def _sb_decode_pages(page_table, q, run0, kc, vc, layer, g, first, count):
    n = page_table.shape[0]
    page = kc.shape[-1]
    row = pl.BlockSpec((None, 1, W_ATT), lambda b, j, pt: (b, 0, 0))
    hcol = pl.BlockSpec((None, N_HEADS, 1), lambda b, j, pt: (b, 0, 0))
    kv_tail = (N_HEADS, HEAD_DIM, page)
    last = first + count - 1
    in_specs = ([row, hcol] + [_page_spec(layer, last, g, i, kv_tail) for i in range(g)]
                + [_page_spec(layer, last, g, i, kv_tail) for i in range(g)])
    return pl.pallas_call(
        functools.partial(_sb_decode_kernel, g=g),
        grid_spec=pltpu.PrefetchScalarGridSpec(
            num_scalar_prefetch=1, grid=(n, count // g), in_specs=in_specs,
            out_specs=[row, hcol],
            scratch_shapes=[pltpu.VMEM((N_HEADS, W_ATT), F32),
                            pltpu.VMEM((N_HEADS, W_ATT), F32),
                            pltpu.VMEM((N_HEADS, 1), F32)]),
        out_shape=[jax.ShapeDtypeStruct((n, 1, W_ATT), F32),
                   jax.ShapeDtypeStruct((n, N_HEADS, 1), F32)],
        compiler_params=_cparams("parallel", "arbitrary"),
        name="sb_decode",
    )(page_table, q, run0, *([kc] * g), *([vc] * g))


def _sb_decode(page_table, q, kc, vc, layer, g):
    n, n_pages = page_table.shape
    near = min(g, n_pages)
    run0 = jnp.zeros((n, N_HEADS, 1), F32)
    o_near, run = _sb_decode_pages(page_table, q, run0, kc, vc, layer, near, n_pages - near, near)
    if near == n_pages:
        return o_near

    def rest():
        return o_near + _sb_decode_pages(page_table, q, run, kc, vc, layer, g, 0,
                                         n_pages - near)[0]

    return lax.cond(jnp.max(run) > SB_SKIP_LOG, rest, lambda: o_near)


def _block_diag(blocks):
    g, r, c = blocks.shape
    eye = jnp.eye(g, dtype=blocks.dtype)
    return jnp.einsum("grc,gh->grhc", blocks, eye).reshape(g * r, g * c)


def _prep_layer(l, norm_g, w_in, b_fgate, qn_g, kn_g, ssm_a_re, ssm_a_im, ssm_log_dt,
                ssm_b_re, ssm_b_im, ssm_c_re, ssm_c_im, ssm_d, w_glu, w_br_fox, w_br_sb,
                w_br_ssm, w_out):
    d = w_in.shape[1]
    wt = jnp.swapaxes(w_in[l], 0, 1)
    off = [0]
    for s in (W_ATT, W_ATT, W_ATT, N_HEADS, W_ATT, W_ATT, W_ATT, W_ATT, W_ATT, W_SSM, W_SSM,
              d, d, d):
        off.append(off[-1] + s)
    seg = lambda i: wt[off[i]:off[i + 1]]
    w_qkv = jnp.concatenate([seg(0), seg(1), seg(2), seg(5), seg(6), seg(7), seg(9)]).astype(BF16)
    w_ff = jnp.concatenate([seg(3), jnp.zeros((LANES - N_HEADS, d), F32)]).astype(BF16)
    w_gate = jnp.concatenate([seg(4), seg(8), seg(10), seg(11), seg(12), seg(13)]).astype(BF16)
    tr = lambda a: jnp.swapaxes(a, 1, 2)
    return dict(
        g_row=norm_g[l][None, :], w_qkv=w_qkv, w_ff=w_ff, w_gate=w_gate,
        bf_col=b_fgate[l][:, None], bf_row=b_fgate[l][None, :],
        qn_col=qn_g[l][:, None], kn_col=kn_g[l][:, None],
        qn_row=jnp.tile(qn_g[l], N_HEADS)[None, :], kn_row=jnp.tile(kn_g[l], N_HEADS)[None, :],
        a_re=ssm_a_re[l].reshape(1, N_STATE), a_im=ssm_a_im[l].reshape(1, N_STATE),
        ldt=jnp.repeat(ssm_log_dt[l], SSM_STATE)[None, :],
        bre=_block_diag(tr(ssm_b_re[l])).astype(BF16),
        bim=_block_diag(tr(ssm_b_im[l])).astype(BF16),
        cre=_block_diag(tr(ssm_c_re[l])).astype(BF16),
        cim=_block_diag(tr(ssm_c_im[l])).astype(BF16),
        d_row=ssm_d[l][None, :], wglu=w_glu[l].astype(BF16),
        w_br=jnp.stack([w_br_fox[l], w_br_sb[l], w_br_ssm[l]]).astype(BF16),
        w_out=w_out[l].astype(BF16))


def kernel(x_prompt, x_sample, cache_fox_k, cache_fox_v, cache_fox_logf, cache_sb_k, cache_sb_v, state_ssm_re, state_ssm_im, page_table, norm_g, w_in, b_fgate, qn_g, kn_g, ssm_a_re, ssm_a_im, ssm_log_dt, ssm_b_re, ssm_b_im, ssm_c_re, ssm_c_im, ssm_d, w_glu, w_br_fox, w_br_sb, w_br_ssm, w_out):
    depth = w_in.shape[0]
    nb, t, d = x_prompt.shape
    ns = x_sample.shape[0]
    to_pages = lambda c: jnp.transpose(c, (0, 1, 3, 4, 2))
    fkc, fvc, skc, svc = (to_pages(c) for c in (cache_fox_k, cache_fox_v, cache_sb_k, cache_sb_v))
    lfc = jnp.transpose(cache_fox_logf, (0, 1, 3, 2))
    headmean = _block_diag(jnp.full((N_HEADS, HEAD_DIM, HEAD_DIM), 1.0 / HEAD_DIM, F32))

    yp = x_prompt
    ys = x_sample.reshape(ns, d)
    ents_p, ents_s = [], []
    for l in range(depth):
        p = _prep_layer(l, norm_g, w_in, b_fgate, qn_g, kn_g, ssm_a_re, ssm_a_im, ssm_log_dt,
                        ssm_b_re, ssm_b_im, ssm_c_re, ssm_c_im, ssm_d, w_glu, w_br_fox,
                        w_br_sb, w_br_ssm, w_out)
        fq, fk, fv, lf, lfn, sq, sk, sv, u = _inproj_prompt(
            yp, p["g_row"], p["w_qkv"], p["w_ff"], p["bf_col"], p["bf_row"], p["qn_col"],
            p["kn_col"], tm=512)
        o_fox = _fox_prompt(fq, fk, fv, lf, lfn, tq=256)
        o_sb = _sb_prompt(sq, sk, sv, tq=256, tk=128)
        u_tm = jnp.swapaxes(u, 0, 1).reshape(t * nb, W_SSM)
        o_tm, hre, him = _ssm_prompt(u_tm, nb, p["a_re"], p["a_im"], p["ldt"], p["bre"],
                                     p["bim"], p["cre"], p["cim"], p["d_row"], p["wglu"], tt=128)
        o_ssm = jnp.swapaxes(o_tm.reshape(t, nb, W_SSM), 0, 1)
        yp = _merge(yp, o_fox, o_sb, o_ssm, p["g_row"], p["w_gate"], p["w_br"], p["w_out"],
                    tm=512, attn_transposed=True)
        ents_p.append((fk, fv, lf, sk, sv, hre, him))
        (dfq, dfk, dfv, dlf, dsq, dsk, dsv, dom, h1r, h1i) = _decode_pre(
            ys, p["g_row"], p["w_qkv"], p["w_ff"], p["bf_row"], p["qn_row"], p["kn_row"],
            headmean, p["a_re"], p["a_im"], p["ldt"], p["bre"], p["bim"], p["cre"], p["cim"],
            p["d_row"], p["wglu"], state_ssm_re[l].reshape(ns, N_STATE),
            state_ssm_im[l].reshape(ns, N_STATE))
        do_fox = _fox_decode(page_table, dfq[:, None, :], dfk[:, None, :], dfv[:, None, :],
                             dlf[:, :, None], fkc, fvc, lfc, l, g=8).reshape(ns, W_ATT)
        do_sb = _sb_decode(page_table, dsq[:, None, :], skc, svc, l, g=4).reshape(ns, W_ATT)
        ys = _merge(ys[None], do_fox[None], do_sb[None], dom[None], p["g_row"], p["w_gate"],
                    p["w_br"], p["w_out"], tm=ns, attn_transposed=False)[0]
        ents_s.append((dfk, dfv, dlf, dsk, dsv, h1r, h1i))

    def prompt_kv(i):
        a = jnp.stack([e[i] for e in ents_p]).reshape(depth, nb, N_HEADS, HEAD_DIM, t)
        return jnp.transpose(a, (0, 1, 4, 2, 3))

    p_logf = jnp.transpose(jnp.stack([e[2] for e in ents_p]), (0, 1, 3, 2))
    p_re = jnp.stack([e[5] for e in ents_p]).reshape(depth, nb, SSM_GROUPS, SSM_STATE)
    p_im = jnp.stack([e[6] for e in ents_p]).reshape(depth, nb, SSM_GROUPS, SSM_STATE)
    s_kv = lambda i: jnp.stack([e[i] for e in ents_s]).reshape(depth, ns, 1, N_HEADS, HEAD_DIM)
    s_logf = jnp.stack([e[2] for e in ents_s]).reshape(depth, ns, 1, N_HEADS)
    s_re = jnp.stack([e[5] for e in ents_s]).reshape(depth, ns, SSM_GROUPS, SSM_STATE)
    s_im = jnp.stack([e[6] for e in ents_s]).reshape(depth, ns, SSM_GROUPS, SSM_STATE)
    return (yp, ys.reshape(ns, 1, d), prompt_kv(0), prompt_kv(1), p_logf, prompt_kv(3),
            prompt_kv(4), p_re, p_im, s_kv(0), s_kv(1), s_logf, s_kv(3), s_kv(4), s_re, s_im)
```

```python
import functools
import math

import jax
import jax.numpy as jnp
from jax import lax
from jax.experimental import pallas as pl
from jax.experimental.pallas import tpu as pltpu

HEAD_DIM = 64
N_HEADS = 8
W_ATT = N_HEADS * HEAD_DIM
SSM_GROUP = 16
SSM_GROUPS = 32
SSM_STATE = 64
W_SSM = SSM_GROUP * SSM_GROUPS
N_STATE = SSM_GROUPS * SSM_STATE
EPS = 1e-6
NEG_INF = -1e30
SCALE = HEAD_DIM ** -0.5
SB_SKIP_LOG = -110.0
FOX_SKIP_LOG = -110.0
FOX_NEAR_STEPS = 3

LANES = 128
V7X_VMEM_BYTES = 64 * 1024 * 1024
VMEM_LIMIT = V7X_VMEM_BYTES - 8 * 1024 * 1024

F32 = jnp.float32
BF16 = jnp.bfloat16


def _cparams(*sem):
    return pltpu.CompilerParams(dimension_semantics=sem, vmem_limit_bytes=VMEM_LIMIT)


def _dot_nn(a, b, precision=None):
    return lax.dot_general(a, b, (((1,), (0,)), ((), ())), precision=precision,
                           preferred_element_type=F32)


def _dot_nt(a, b):
    return lax.dot_general(a, b, (((1,), (1,)), ((), ())), preferred_element_type=F32)


def _dot_tn(a, b):
    return lax.dot_general(a, b, (((0,), (0,)), ((), ())), preferred_element_type=F32)


def _log_sigmoid(x):
    return jnp.minimum(x, 0.0) - jnp.log(1.0 + jnp.exp(-jnp.abs(x)))


def _softplus(x):
    return jnp.maximum(x, 0.0) + jnp.log(1.0 + jnp.exp(-jnp.abs(x)))


def _sigmoid(x):
    return jax.nn.sigmoid(x)


def _gelu_tanh(x):
    c = math.sqrt(2.0 / math.pi)
    return x * (0.5 * (1.0 + jnp.tanh(c * (x + 0.044715 * (x * x * x)))))


def _rms_rows(x, g):
    ms = jnp.mean(x * x, axis=-1, keepdims=True)
    return x * lax.rsqrt(ms + EPS) * g


def _headnorm_t(xt, g_col):
    n = xt.shape[-1]
    x3 = xt.reshape(N_HEADS, HEAD_DIM, n)
    ms = jnp.mean(x3 * x3, axis=1, keepdims=True)
    y = x3 * lax.rsqrt(ms + EPS) * g_col[None]
    return y.reshape(W_ATT, n)


def _inproj_prompt_kernel(x_ref, g_ref, w_ref, wff_ref, bfc_ref, bfr_ref, qn_ref, kn_ref,
                          fq_ref, fk_ref, fv_ref, lf_ref, lfn_ref, sq_ref, sk_ref, sv_ref,
                          u_ref):
    h = _rms_rows(x_ref[...], g_ref[...]).astype(BF16)

    def proj_t(i):
        return _dot_nt(w_ref[i * W_ATT:(i + 1) * W_ATT, :], h)

    fq_ref[...] = (_headnorm_t(proj_t(0), qn_ref[...]) * SCALE).astype(BF16)
    fk_ref[...] = _headnorm_t(proj_t(1), kn_ref[...])
    fv_ref[...] = proj_t(2)
    wff = wff_ref[...]
    lf_ref[...] = _log_sigmoid(_dot_nt(wff, h)[:N_HEADS, :] + bfc_ref[...])
    lfn_ref[...] = _log_sigmoid(_dot_nt(h, wff)[:, :N_HEADS] + bfr_ref[...])
    sq_ref[...] = (proj_t(3) * SCALE).astype(BF16)
    sk_ref[...] = proj_t(4)
    sv_ref[...] = proj_t(5)
    u_ref[...] = _dot_nt(h, w_ref[6 * W_ATT:7 * W_ATT, :])


def _inproj_prompt(x, g_row, w_qkv, w_ff, bf_col, bf_row, qn_col, kn_col, tm):
    b, t, d = x.shape
    tm = min(tm, t)
    full = lambda a: pl.BlockSpec(a.shape, lambda i, j: (0,) * a.ndim)
    tspec = pl.BlockSpec((None, W_ATT, tm), lambda i, j: (i, 0, j))
    kv = jax.ShapeDtypeStruct((b, W_ATT, t), F32)
    qs = jax.ShapeDtypeStruct((b, W_ATT, t), BF16)
    return pl.pallas_call(
        _inproj_prompt_kernel,
        grid=(b, t // tm),
        in_specs=[pl.BlockSpec((None, tm, d), lambda i, j: (i, j, 0)), full(g_row), full(w_qkv),
                  full(w_ff), full(bf_col), full(bf_row), full(qn_col), full(kn_col)],
        out_specs=[tspec, tspec, tspec,
                   pl.BlockSpec((None, N_HEADS, tm), lambda i, j: (i, 0, j)),
                   pl.BlockSpec((None, tm, N_HEADS), lambda i, j: (i, j, 0)),
                   tspec, tspec, tspec,
                   pl.BlockSpec((None, tm, W_SSM), lambda i, j: (i, j, 0))],
        out_shape=[qs, kv, kv, jax.ShapeDtypeStruct((b, N_HEADS, t), F32),
                   jax.ShapeDtypeStruct((b, t, N_HEADS), F32), qs, kv, kv,
                   jax.ShapeDtypeStruct((b, t, W_SSM), F32)],
        compiler_params=_cparams("parallel", "parallel"),
        name="inproj_prompt",
    )(x, g_row, w_qkv, w_ff, bf_col, bf_row, qn_col, kn_col)


def _fox_prompt_kernel(q_ref, k_ref, v_ref, lf_ref, lfn_ref, o_ref,
                       kb_ref, vb_ref, crow_ref, cb_ref, acc_ref, m_ref, l_ref, a_ref, s_ref,
                       p_ref, *, tq, cb):
    qi = pl.program_id(1)
    t = k_ref.shape[-1]

    @pl.when(qi == 0)
    def _():
        kb_ref[...] = k_ref[...].astype(BF16)
        vb_ref[...] = v_ref[...].astype(BF16)
        r = lax.broadcasted_iota(jnp.int32, (cb, cb), 0)
        c = lax.broadcasted_iota(jnp.int32, (cb, cb), 1)
        upper = (c > r).astype(F32)
        lower = (r > c).astype(F32)
        carry_c = jnp.zeros((1, N_HEADS), F32)
        carry_r = jnp.zeros((N_HEADS, 1), F32)
        for blk in reversed(range(t // cb)):
            seg_c = lfn_ref[blk * cb:(blk + 1) * cb, :]
            within_c = _dot_nn(upper, seg_c, precision=lax.Precision.HIGHEST)
            c_col = -(within_c + carry_c)
            for h in range(N_HEADS):
                cb_ref[h, blk * cb:(blk + 1) * cb, :] = jnp.broadcast_to(c_col[:, h:h + 1],
                                                                         (cb, LANES))
            carry_c = carry_c + jnp.sum(seg_c, axis=0, keepdims=True)
            seg_r = lf_ref[:, blk * cb:(blk + 1) * cb]
            within_r = _dot_nn(seg_r, lower, precision=lax.Precision.HIGHEST)
            crow_ref[:, blk * cb:(blk + 1) * cb] = -(within_r + carry_r)
            carry_r = carry_r + jnp.sum(seg_r, axis=1, keepdims=True)

    m_ref[...] = jnp.full(m_ref.shape, NEG_INF, F32)
    l_ref[...] = jnp.zeros_like(l_ref)
    acc_ref[...] = jnp.zeros_like(acc_ref)
    q0 = pl.multiple_of(qi * tq, tq)

    def tile(j, masked):
        k0 = pl.multiple_of(j * tq, tq)
        heads = [slice(h * HEAD_DIM, (h + 1) * HEAD_DIM) for h in range(N_HEADS)]
        for h, rows in enumerate(heads):
            s_ref[h] = _dot_tn(kb_ref[rows, pl.ds(k0, tq)], q_ref[rows, :])
        for h in range(N_HEADS):
            ck = cb_ref[h, pl.ds(k0, tq), :]
            ck = jnp.concatenate([ck] * (tq // LANES), axis=1)
            s = s_ref[h] + (crow_ref[h:h + 1, pl.ds(q0, tq)] - ck)
            if masked:
                kpos = lax.broadcasted_iota(jnp.int32, (tq, tq), 0)
                qpos = lax.broadcasted_iota(jnp.int32, (tq, tq), 1)
                s = jnp.where(kpos <= qpos, s, NEG_INF)
            m_old = m_ref[h:h + 1, :]
            m_new = jnp.maximum(m_old, jnp.max(s, axis=0, keepdims=True))
            p = jnp.exp(s - m_new)
            alpha = jnp.exp(m_old - m_new)
            l_ref[h:h + 1, :] = alpha * l_ref[h:h + 1, :] + jnp.sum(p, axis=0, keepdims=True)
            m_ref[h:h + 1, :] = m_new
            a_ref[h:h + 1, :] = alpha
            p_ref[h] = p.astype(BF16)
        for h, rows in enumerate(heads):
            acc_ref[rows, :] = (a_ref[h:h + 1, :] * acc_ref[rows, :]
                                + _dot_nn(vb_ref[rows, pl.ds(k0, tq)], p_ref[h]))

    def body(j, carry):
        tile(j, False)
        return carry

    lax.fori_loop(0, qi, body, 0)
    tile(qi, True)
    for h in range(N_HEADS):
        rows = slice(h * HEAD_DIM, (h + 1) * HEAD_DIM)
        o_ref[rows, :] = acc_ref[rows, :] / l_ref[h:h + 1, :]


def _fox_prompt(q_t, k_t, v_t, lf_t, lf_n, tq):
    b, _, t = k_t.shape
    tq = min(tq, t)
    whole = pl.BlockSpec((None, W_ATT, t), lambda i, j: (i, 0, 0))
    return pl.pallas_call(
        functools.partial(_fox_prompt_kernel, tq=tq, cb=min(256, t)),
        grid=(b, t // tq),
        in_specs=[pl.BlockSpec((None, W_ATT, tq), lambda i, j: (i, 0, j)), whole, whole,
                  pl.BlockSpec((None, N_HEADS, t), lambda i, j: (i, 0, 0)),
                  pl.BlockSpec((None, t, N_HEADS), lambda i, j: (i, 0, 0))],
        out_specs=pl.BlockSpec((None, W_ATT, tq), lambda i, j: (i, 0, j)),
        out_shape=jax.ShapeDtypeStruct((b, W_ATT, t), F32),
        scratch_shapes=[pltpu.VMEM((W_ATT, t), BF16), pltpu.VMEM((W_ATT, t), BF16),
                        pltpu.VMEM((N_HEADS, t), F32), pltpu.VMEM((N_HEADS, t, LANES), F32),
                        pltpu.VMEM((W_ATT, tq), F32), pltpu.VMEM((N_HEADS, tq), F32),
                        pltpu.VMEM((N_HEADS, tq), F32), pltpu.VMEM((N_HEADS, tq), F32),
                        pltpu.VMEM((N_HEADS, tq, tq), F32), pltpu.VMEM((N_HEADS, tq, tq), BF16)],
        compiler_params=_cparams("parallel", "arbitrary"),
        name="fox_prompt",
    )(q_t, k_t, v_t, lf_t, lf_n)


def _sb_prompt_kernel(q_ref, k_ref, v_ref, o_ref, kb_ref, vb_ref, acc_ref, run_ref, sum_ref,
                      z_ref, in_ref, hi_ref, lo_ref, w_ref, *, tq, tk, skip_log):
    qi = pl.program_id(1)
    q0 = qi * tq
    nk = tq // tk

    @pl.when(qi == 0)
    def _():
        kb_ref[...] = k_ref[...].astype(BF16)
        vb_ref[...] = v_ref[...].astype(BF16)

    acc_ref[...] = jnp.zeros_like(acc_ref)
    run_ref[...] = jnp.zeros_like(run_ref)
    r = lax.broadcasted_iota(jnp.int32, (tk, tk), 0)
    c = lax.broadcasted_iota(jnp.int32, (tk, tk), 1)
    upper = (c > r).astype(BF16)

    def tile(kj, masked):
        k0 = pl.multiple_of(kj * tk, tk)
        heads = [slice(h * HEAD_DIM, (h + 1) * HEAD_DIM) for h in range(N_HEADS)]
        if masked:
            kpos = k0 + lax.broadcasted_iota(jnp.int32, (tk, tq), 0)
            qpos = q0 + lax.broadcasted_iota(jnp.int32, (tk, tq), 1)
            mask = kpos < qpos
        for h, rows in enumerate(heads):
            z_ref[h] = _dot_tn(kb_ref[rows, pl.ds(k0, tk)], q_ref[rows, :])
        for h in range(N_HEADS):
            z = z_ref[h]
            sp = _softplus(z)
            lk = jnp.where(mask, -sp, 0.0) if masked else -sp
            lk_hi = lk.astype(BF16)
            hi_ref[h] = lk_hi
            lo_ref[h] = (lk - lk_hi.astype(F32)).astype(BF16)
            z_ref[h] = z - sp
            sum_ref[h:h + 1, :] = jnp.sum(lk, axis=0, keepdims=True)
        for h in range(N_HEADS):
            in_ref[h] = _dot_nn(upper, hi_ref[h]) + _dot_nn(upper, lo_ref[h])
        for h in range(N_HEADS):
            run = run_ref[h:h + 1, :]
            w = jnp.exp(z_ref[h] + (in_ref[h] + run))
            if masked:
                w = jnp.where(mask, w, 0.0)
            w_ref[h] = w.astype(BF16)
            run_ref[h:h + 1, :] = run + sum_ref[h:h + 1, :]
        for h, rows in enumerate(heads):
            acc_ref[rows, :] += _dot_nn(vb_ref[rows, pl.ds(k0, tk)], w_ref[h])

    for j in range(nk):
        tile(qi * nk + (nk - 1) - j, True)

    def live():
        return (jnp.max(run_ref[...]) > skip_log).astype(jnp.int32)

    def cond(carry):
        kj, alive = carry
        return jnp.logical_and(kj >= 0, alive > 0)

    def body(carry):
        kj, _ = carry
        tile(kj, False)
        return kj - 1, live()

    lax.while_loop(cond, body, (qi * nk - 1, live()))
    o_ref[...] = acc_ref[...]


def _sb_prompt(q_t, k_t, v_t, tq, tk):
    b, _, t = k_t.shape
    tq = min(tq, t)
    tk = min(tk, tq)
    whole = pl.BlockSpec((None, W_ATT, t), lambda i, j: (i, 0, 0))
    return pl.pallas_call(
        functools.partial(_sb_prompt_kernel, tq=tq, tk=tk, skip_log=SB_SKIP_LOG),
        grid=(b, t // tq),
        in_specs=[pl.BlockSpec((None, W_ATT, tq), lambda i, j: (i, 0, j)), whole, whole],
        out_specs=pl.BlockSpec((None, W_ATT, tq), lambda i, j: (i, 0, j)),
        out_shape=jax.ShapeDtypeStruct((b, W_ATT, t), F32),
        scratch_shapes=[pltpu.VMEM((W_ATT, t), BF16), pltpu.VMEM((W_ATT, t), BF16),
                        pltpu.VMEM((W_ATT, tq), F32), pltpu.VMEM((N_HEADS, tq), F32),
                        pltpu.VMEM((N_HEADS, tq), F32),
                        pltpu.VMEM((N_HEADS, tk, tq), F32), pltpu.VMEM((N_HEADS, tk, tq), F32),
                        pltpu.VMEM((N_HEADS, tk, tq), BF16), pltpu.VMEM((N_HEADS, tk, tq), BF16),
                        pltpu.VMEM((N_HEADS, tk, tq), BF16)],
        compiler_params=_cparams("parallel", "arbitrary"),
        name="sb_prompt",
    )(q_t, k_t, v_t)


def _ssm_discretise(a_re, a_im, log_dt):
    dt = jnp.exp(log_dt)
    mag = jnp.exp(a_re * dt)
    ang = a_im * dt
    lb_re = mag * jnp.cos(ang)
    lb_im = mag * jnp.sin(ang)
    den = a_re * a_re + a_im * a_im
    nr = lb_re - 1.0
    coef_re = (nr * a_re + lb_im * a_im) / den
    coef_im = (lb_im * a_re - nr * a_im) / den
    return lb_re, lb_im, coef_re, coef_im


def _ssm_readout(h_re, h_im, u, cre_ref, cim_ref, d_row, wglu_ref):
    half_s, half_w = N_STATE // 2, W_SSM // 2
    hr = h_re.astype(BF16)
    hi = h_im.astype(BF16)
    y_lo = (_dot_nn(hr[:, :half_s], cre_ref[:half_s, :half_w])
            - _dot_nn(hi[:, :half_s], cim_ref[:half_s, :half_w]))
    y_hi = (_dot_nn(hr[:, half_s:], cre_ref[half_s:, half_w:])
            - _dot_nn(hi[:, half_s:], cim_ref[half_s:, half_w:]))
    y = jnp.concatenate([y_lo, y_hi], axis=1) + d_row * u
    y = _gelu_tanh(y)
    return y * _sigmoid(_dot_nn(y.astype(BF16), wglu_ref[...]))


def _ssm_prompt_kernel(u_ref, are_ref, aim_ref, ldt_ref, bre_ref, bim_ref, cre_ref, cim_ref,
                       d_ref, wglu_ref, o_ref, hre_ref, him_ref,
                       xre_ref, xim_ref, par_ref, hst_ref, *, nb, tt, lane_chunk):
    i = pl.program_id(0)

    @pl.when(i == 0)
    def _():
        pars = _ssm_discretise(are_ref[...], aim_ref[...], ldt_ref[...])
        for k in range(4):
            par_ref[k] = jnp.broadcast_to(pars[k], (nb, N_STATE))
        hst_ref[...] = jnp.zeros_like(hst_ref)

    ub = u_ref[...].astype(BF16)
    half_s, half_w = N_STATE // 2, W_SSM // 2
    xre_ref[:, :half_s] = _dot_nn(ub[:, :half_w], bre_ref[:half_w, :half_s])
    xre_ref[:, half_s:] = _dot_nn(ub[:, half_w:], bre_ref[half_w:, half_s:])
    xim_ref[:, :half_s] = _dot_nn(ub[:, :half_w], bim_ref[:half_w, :half_s])
    xim_ref[:, half_s:] = _dot_nn(ub[:, half_w:], bim_ref[half_w:, half_s:])

    for c in range(N_STATE // lane_chunk):
        cols = slice(c * lane_chunk, (c + 1) * lane_chunk)
        lb_re, lb_im = par_ref[0, :, cols], par_ref[1, :, cols]
        cf_re, cf_im = par_ref[2, :, cols], par_ref[3, :, cols]

        def body(t, carry):
            h_re, h_im = carry
            rws = pl.ds(pl.multiple_of(t * nb, nb), nb)
            b_re, b_im = xre_ref[rws, cols], xim_ref[rws, cols]
            x_re = cf_re * b_re - cf_im * b_im
            x_im = cf_re * b_im + cf_im * b_re
            n_re = lb_re * h_re - lb_im * h_im + x_re
            n_im = lb_re * h_im + lb_im * h_re + x_im
            xre_ref[rws, cols] = n_re
            xim_ref[rws, cols] = n_im
            return n_re, n_im

        h_re, h_im = lax.fori_loop(0, tt, body, (hst_ref[0, :, cols], hst_ref[1, :, cols]),
                                   unroll=8)
        hst_ref[0, :, cols] = h_re
        hst_ref[1, :, cols] = h_im

    o_ref[...] = _ssm_readout(xre_ref[...], xim_ref[...], u_ref[...], cre_ref, cim_ref,
                              d_ref[...], wglu_ref)
    hre_ref[...] = hst_ref[0]
    him_ref[...] = hst_ref[1]


def _ssm_prompt(u_tm, nb, a_re, a_im, ldt, bre, bim, cre, cim, d_row, wglu, tt):
    t = u_tm.shape[0] // nb
    tt = min(tt, t)
    full = lambda a: pl.BlockSpec(a.shape, lambda i: (0,) * a.ndim)
    st = jax.ShapeDtypeStruct((nb, N_STATE), F32)
    return pl.pallas_call(
        functools.partial(_ssm_prompt_kernel, nb=nb, tt=tt, lane_chunk=512),
        grid=(t // tt,),
        in_specs=[pl.BlockSpec((nb * tt, W_SSM), lambda i: (i, 0)), full(a_re), full(a_im),
                  full(ldt), full(bre), full(bim), full(cre), full(cim), full(d_row), full(wglu)],
        out_specs=[pl.BlockSpec((nb * tt, W_SSM), lambda i: (i, 0)),
                   pl.BlockSpec((nb, N_STATE), lambda i: (0, 0)),
                   pl.BlockSpec((nb, N_STATE), lambda i: (0, 0))],
        out_shape=[jax.ShapeDtypeStruct((t * nb, W_SSM), F32), st, st],
        scratch_shapes=[pltpu.VMEM((nb * tt, N_STATE), F32), pltpu.VMEM((nb * tt, N_STATE), F32),
                        pltpu.VMEM((4, nb, N_STATE), F32), pltpu.VMEM((2, nb, N_STATE), F32)],
        compiler_params=_cparams("arbitrary"),
        name="ssm_prompt",
    )(u_tm, a_re, a_im, ldt, bre, bim, cre, cim, d_row, wglu)


def _merge_kernel(x_ref, of_ref, os_ref, om_ref, g_ref, wg_ref, wbr_ref, wo_ref, y_ref,
                  *, attn_transposed):
    x = x_ref[...]
    h = _rms_rows(x, g_ref[...]).astype(BF16)

    def proj(lo, n):
        return _dot_nt(h, wg_ref[lo:lo + n, :])

    def branch(o, z_lo, wi, g_lo):
        z = proj(z_lo, W_ATT)
        a = (o * (z * _sigmoid(z))).astype(BF16)
        br = _dot_nn(a, wbr_ref[wi])
        return _sigmoid(proj(g_lo, x.shape[-1])) * br

    of = of_ref[...].T if attn_transposed else of_ref[...]
    os_ = os_ref[...].T if attn_transposed else os_ref[...]
    d = x.shape[-1]
    m = (branch(of, 0, 0, 3 * W_ATT) + branch(os_, W_ATT, 1, 3 * W_ATT + d)
         + branch(om_ref[...], 2 * W_ATT, 2, 3 * W_ATT + 2 * d))
    y_ref[...] = x + _dot_nn(m.astype(BF16), wo_ref[...])


def _merge(x, of, os_, om, g_row, w_gate, w_br, w_out, tm, attn_transposed):
    b, t, d = x.shape
    tm = min(tm, t)
    full = lambda a: pl.BlockSpec(a.shape, lambda i, j: (0,) * a.ndim)
    rspec = lambda n: pl.BlockSpec((None, tm, n), lambda i, j: (i, j, 0))
    aspec = pl.BlockSpec((None, W_ATT, tm), lambda i, j: (i, 0, j)) if attn_transposed \
        else rspec(W_ATT)
    return pl.pallas_call(
        functools.partial(_merge_kernel, attn_transposed=attn_transposed),
        grid=(b, t // tm),
        in_specs=[rspec(d), aspec, aspec, rspec(W_SSM), full(g_row), full(w_gate), full(w_br),
                  full(w_out)],
        out_specs=rspec(d),
        out_shape=jax.ShapeDtypeStruct((b, t, d), F32),
        compiler_params=_cparams("parallel", "parallel"),
        name="merge_t" if attn_transposed else "merge_n",
    )(x, of, os_, om, g_row, w_gate, w_br, w_out)


def _decode_pre_kernel(x_ref, g_ref, w_ref, wff_ref, bfr_ref, qn_ref, kn_ref, hd_ref,
                       are_ref, aim_ref, ldt_ref, bre_ref, bim_ref, cre_ref, cim_ref, d_ref,
                       wglu_ref, h0r_ref, h0i_ref,
                       fq_ref, fk_ref, fv_ref, lf_ref, sq_ref, sk_ref, sv_ref, om_ref,
                       h1r_ref, h1i_ref):
    h = _rms_rows(x_ref[...], g_ref[...]).astype(BF16)

    def proj(i):
        return _dot_nt(h, w_ref[i * W_ATT:(i + 1) * W_ATT, :])

    def headnorm(x, g_row):
        ms = _dot_nn(x * x, hd_ref[...], precision=lax.Precision.HIGHEST)
        return x * lax.rsqrt(ms + EPS) * g_row

    fq_ref[...] = headnorm(proj(0), qn_ref[...]) * SCALE
    fk_ref[...] = headnorm(proj(1), kn_ref[...])
    fv_ref[...] = proj(2)
    lf_ref[...] = _log_sigmoid(_dot_nt(h, wff_ref[...])[:, :N_HEADS] + bfr_ref[...])
    sq_ref[...] = proj(3) * SCALE
    sk_ref[...] = proj(4)
    sv_ref[...] = proj(5)
    u = proj(6)
    lb_re, lb_im, cf_re, cf_im = _ssm_discretise(are_ref[...], aim_ref[...], ldt_ref[...])
    ub = u.astype(BF16)
    b_re = _dot_nn(ub, bre_ref[...])
    b_im = _dot_nn(ub, bim_ref[...])
    x_re = cf_re * b_re - cf_im * b_im
    x_im = cf_re * b_im + cf_im * b_re
    r0, i0 = h0r_ref[...], h0i_ref[...]
    h_re = x_re + lb_re * r0 - lb_im * i0
    h_im = x_im + lb_re * i0 + lb_im * r0
    h1r_ref[...] = h_re
    h1i_ref[...] = h_im
    om_ref[...] = _ssm_readout(h_re, h_im, u, cre_ref, cim_ref, d_ref[...], wglu_ref)


def _decode_pre(x, g_row, w_qkv, w_ff, bf_row, qn_row, kn_row, headmean, a_re, a_im, ldt,
                bre, bim, cre, cim, d_row, wglu, h0r, h0i):
    n = x.shape[0]
    att = jax.ShapeDtypeStruct((n, W_ATT), F32)
    st = jax.ShapeDtypeStruct((n, N_STATE), F32)
    return pl.pallas_call(
        _decode_pre_kernel,
        out_shape=[att, att, att, jax.ShapeDtypeStruct((n, N_HEADS), F32), att, att, att,
                   jax.ShapeDtypeStruct((n, W_SSM), F32), st, st],
        compiler_params=pltpu.CompilerParams(vmem_limit_bytes=VMEM_LIMIT),
        name="decode_pre",
    )(x, g_row, w_qkv, w_ff, bf_row, qn_row, kn_row, headmean, a_re, a_im, ldt, bre, bim,
      cre, cim, d_row, wglu, h0r, h0i)


def _suffix_sums(blocks):
    x = jnp.concatenate(blocks, axis=0)
    r = lax.broadcasted_iota(jnp.int32, (LANES, LANES), 0)
    c = lax.broadcasted_iota(jnp.int32, (LANES, LANES), 1)
    y = _dot_nn(x, (r > c).astype(F32), precision=lax.Precision.HIGHEST)
    return [y[i * N_HEADS:(i + 1) * N_HEADS] for i in range(len(blocks))]


def _head_block_rows(row):
    head = lax.broadcasted_iota(jnp.int32, (N_HEADS, W_ATT), 0)
    lane = lax.broadcasted_iota(jnp.int32, (N_HEADS, W_ATT), 1)
    return jnp.where(lane // HEAD_DIM == head, jnp.broadcast_to(row, (N_HEADS, W_ATT)), 0.0)


def _page_scores(qblk, k_ref):
    return _dot_nn(qblk.astype(BF16), k_ref[...].reshape(W_ATT, k_ref.shape[-1]).astype(BF16))


def _weights_times_pages(weights, v_refs):
    page = v_refs[0].shape[-1]
    v_all = jnp.concatenate([v[...].reshape(W_ATT, page).astype(BF16) for v in v_refs], axis=1)
    w_all = jnp.concatenate(weights, axis=1).astype(BF16)
    return _dot_nt(w_all, v_all)


def _head_rows_to_row(x):
    head = lax.broadcasted_iota(jnp.int32, x.shape, 0)
    lane = lax.broadcasted_iota(jnp.int32, x.shape, 1)
    return jnp.sum(jnp.where(lane // HEAD_DIM == head, x, 0.0), axis=0, keepdims=True)


def _per_head_to_row(col):
    return _head_rows_to_row(jnp.broadcast_to(col, (N_HEADS, W_ATT)))


def _fox_scores_kernel(pt_ref, q_ref, kn_ref, lfn_ref, *refs, g, near_steps):
    k_refs, lf_refs = refs[:g], refs[g:2 * g]
    s_ref, m_ref, mfar_ref, qblk_ref, run_ref = refs[2 * g:]
    j = pl.program_id(1)
    page = k_refs[0].shape[-1]

    @pl.when(j == 0)
    def _():
        qblk = _head_block_rows(q_ref[...])
        qblk_ref[...] = qblk
        m_ref[...] = jnp.sum(qblk * kn_ref[...], axis=1, keepdims=True)
        mfar_ref[...] = jnp.full(mfar_ref.shape, NEG_INF, F32)
        run_ref[...] = lfn_ref[...]

    lfs = [lf_refs[i][...] for i in range(g)]
    suffix = _suffix_sums(lfs)
    run = run_ref[...]
    m_step = jnp.full(m_ref.shape, NEG_INF, F32)
    for i in range(g):
        s = _page_scores(qblk_ref[...], k_refs[i]) + (run + suffix[i])
        run = run + jnp.sum(lfs[i], axis=1, keepdims=True)
        m_step = jnp.maximum(m_step, jnp.max(s, axis=1, keepdims=True))
        s_ref[:, i * page:(i + 1) * page] = s
    run_ref[...] = run
    m_ref[...] = jnp.maximum(m_ref[...], m_step)

    @pl.when(j >= near_steps)
    def _():
        mfar_ref[...] = jnp.maximum(mfar_ref[...], m_step)


def _fox_values_kernel(pt_ref, q_ref, kn_ref, vn_ref, s_ref, m_ref, *refs, g):
    v_refs = refs[:g]
    o_ref, acc_ref, l_ref = refs[g:]
    j = pl.program_id(1)
    page = v_refs[0].shape[-1]
    m = m_ref[...]

    @pl.when(j == 0)
    def _():
        s_self = jnp.sum(_head_block_rows(q_ref[...]) * kn_ref[...], axis=1, keepdims=True)
        w_self = jnp.exp(s_self - m)
        l_ref[...] = w_self
        acc_ref[...] = w_self * jnp.broadcast_to(vn_ref[...], acc_ref.shape)

    probs = [jnp.exp(s_ref[:, i * page:(i + 1) * page] - m) for i in range(g)]
    l_new = l_ref[...]
    for p in probs:
        l_new = l_new + jnp.sum(p, axis=1, keepdims=True)
    l_ref[...] = l_new
    acc_ref[...] += _weights_times_pages(probs, v_refs)

    @pl.when(j == pl.num_programs(1) - 1)
    def _():
        o_ref[...] = _head_rows_to_row(acc_ref[...]) / _per_head_to_row(l_ref[...])


def _sb_decode_kernel(pt_ref, q_ref, run0_ref, *refs, g):
    k_refs, v_refs = refs[:g], refs[g:2 * g]
    o_ref, runo_ref, qblk_ref, acc_ref, run_ref = refs[2 * g:]
    j = pl.program_id(1)

    @pl.when(j == 0)
    def _():
        qblk_ref[...] = _head_block_rows(q_ref[...])
        acc_ref[...] = jnp.zeros_like(acc_ref)
        run_ref[...] = run0_ref[...]

    zs = [_page_scores(qblk_ref[...], k_refs[i]) for i in range(g)]
    sps = [_softplus(z) for z in zs]
    suffix = _suffix_sums([-sp for sp in sps])
    run = run_ref[...]
    weights = []
    for i in range(g):
        weights.append(jnp.exp((zs[i] - sps[i]) + (suffix[i] + run)))
        run = run - jnp.sum(sps[i], axis=1, keepdims=True)
    run_ref[...] = run
    acc_ref[...] += _weights_times_pages(weights, v_refs)

    @pl.when(j == pl.num_programs(1) - 1)
    def _():
        o_ref[...] = _head_rows_to_row(acc_ref[...])
        runo_ref[...] = run_ref[...]


def _page_spec(layer, last_page, g, i, tail, step_of=lambda j: j):
    def index(b, j, pt):
        return (layer, pt[b, last_page - (step_of(j) * g + i)]) + (0,) * len(tail)
    return pl.BlockSpec((None, None) + tail, index)


def _fox_decode(page_table, q_row, k_new_row, v_new_row, lf_new, kc, vc, lfc, layer, g):
    n, n_pages = page_table.shape
    page = kc.shape[-1]
    g = min(g, n_pages)
    steps = n_pages // g
    near_steps = min(FOX_NEAR_STEPS, steps)
    row = pl.BlockSpec((None, 1, W_ATT), lambda b, j, pt: (b, 0, 0))
    hcol = pl.BlockSpec((None, N_HEADS, 1), lambda b, j, pt: (b, 0, 0))
    sblk = pl.BlockSpec((None, N_HEADS, g * page), lambda b, j, pt: (b, 0, j))
    kv_tail = (N_HEADS, HEAD_DIM, page)
    last = n_pages - 1
    stat = jax.ShapeDtypeStruct((n, N_HEADS, 1), F32)
    s, m, m_far = pl.pallas_call(
        functools.partial(_fox_scores_kernel, g=g, near_steps=near_steps),
        grid_spec=pltpu.PrefetchScalarGridSpec(
            num_scalar_prefetch=1, grid=(n, steps),
            in_specs=([row, row, hcol]
                      + [_page_spec(layer, last, g, i, kv_tail) for i in range(g)]
                      + [_page_spec(layer, last, g, i, (N_HEADS, page)) for i in range(g)]),
            out_specs=[sblk, hcol, hcol],
            scratch_shapes=[pltpu.VMEM((N_HEADS, W_ATT), F32), pltpu.VMEM((N_HEADS, 1), F32)]),
        out_shape=[jax.ShapeDtypeStruct((n, N_HEADS, n_pages * page), F32), stat, stat],
        compiler_params=_cparams("parallel", "arbitrary"),
        name="fox_scores",
    )(page_table, q_row, k_new_row, lf_new, *([kc] * g), *([lfc] * g))

    def values(n_steps):
        return pl.pallas_call(
            functools.partial(_fox_values_kernel, g=g),
            grid_spec=pltpu.PrefetchScalarGridSpec(
                num_scalar_prefetch=1, grid=(n, n_steps),
                in_specs=([row, row, row, sblk, hcol]
                          + [_page_spec(layer, last, g, i, kv_tail) for i in range(g)]),
                out_specs=row,
                scratch_shapes=[pltpu.VMEM((N_HEADS, W_ATT), F32),
                                pltpu.VMEM((N_HEADS, 1), F32)]),
            out_shape=jax.ShapeDtypeStruct((n, 1, W_ATT), F32),
            compiler_params=_cparams("parallel", "arbitrary"),
            name="fox_values",
        )(page_table, q_row, k_new_row, v_new_row, s, m, *([vc] * g))

    if near_steps == steps:
        return values(steps)
    return lax.cond(jnp.max(m_far - m) > FOX_SKIP_LOG, lambda: values(steps),
                    lambda: values(near_steps))


def _sb_decode_pages(page_table, q, run0, kc, vc, layer, g, first, count):
    n = page_table.shape[0]
    page = kc.shape[-1]
    row = pl.BlockSpec((None, 1, W_ATT), lambda b, j, pt: (b, 0, 0))
    hcol = pl.BlockSpec((None, N_HEADS, 1), lambda b, j, pt: (b, 0, 0))
    kv_tail = (N_HEADS, HEAD_DIM, page)
    last = first + count - 1
    in_specs = ([row, hcol] + [_page_spec(layer, last, g, i, kv_tail) for i in range(g)]
                + [_page_spec(layer, last, g, i, kv_tail) for i in range(g)])
    return pl.pallas_call(
        functools.partial(_sb_decode_kernel, g=g),
        grid_spec=pltpu.PrefetchScalarGridSpec(
            num_scalar_prefetch=1, grid=(n, count // g), in_specs=in_specs,
            out_specs=[row, hcol],
            scratch_shapes=[pltpu.VMEM((N_HEADS, W_ATT), F32),
                            pltpu.VMEM((N_HEADS, W_ATT), F32),
                            pltpu.VMEM((N_HEADS, 1), F32)]),
        out_shape=[jax.ShapeDtypeStruct((n, 1, W_ATT), F32),
                   jax.ShapeDtypeStruct((n, N_HEADS, 1), F32)],
        compiler_params=_cparams("parallel", "arbitrary"),
        name="sb_decode",
    )(page_table, q, run0, *([kc] * g), *([vc] * g))


def _sb_decode(page_table, q, kc, vc, layer, g):
    n, n_pages = page_table.shape
    near = min(g, n_pages)
    run0 = jnp.zeros((n, N_HEADS, 1), F32)
    o_near, run = _sb_decode_pages(page_table, q, run0, kc, vc, layer, near, n_pages - near, near)
    if near == n_pages:
        return o_near

    def rest():
        return o_near + _sb_decode_pages(page_table, q, run, kc, vc, layer, g, 0,
                                         n_pages - near)[0]

    return lax.cond(jnp.max(run) > SB_SKIP_LOG, rest, lambda: o_near)


def _block_diag(blocks):
    g, r, c = blocks.shape
    eye = jnp.eye(g, dtype=blocks.dtype)
    return jnp.einsum("grc,gh->grhc", blocks, eye).reshape(g * r, g * c)


def _prep_layer(l, norm_g, w_in, b_fgate, qn_g, kn_g, ssm_a_re, ssm_a_im, ssm_log_dt,
                ssm_b_re, ssm_b_im, ssm_c_re, ssm_c_im, ssm_d, w_glu, w_br_fox, w_br_sb,
                w_br_ssm, w_out):
    d = w_in.shape[1]
    wt = jnp.swapaxes(w_in[l], 0, 1)
    off = [0]
    for s in (W_ATT, W_ATT, W_ATT, N_HEADS, W_ATT, W_ATT, W_ATT, W_ATT, W_ATT, W_SSM, W_SSM,
              d, d, d):
        off.append(off[-1] + s)
    seg = lambda i: wt[off[i]:off[i + 1]]
    w_qkv = jnp.concatenate([seg(0), seg(1), seg(2), seg(5), seg(6), seg(7), seg(9)]).astype(BF16)
    w_ff = jnp.concatenate([seg(3), jnp.zeros((LANES - N_HEADS, d), F32)]).astype(BF16)
    w_gate = jnp.concatenate([seg(4), seg(8), seg(10), seg(11), seg(12), seg(13)]).astype(BF16)
    tr = lambda a: jnp.swapaxes(a, 1, 2)
    return dict(
        g_row=norm_g[l][None, :], w_qkv=w_qkv, w_ff=w_ff, w_gate=w_gate,
        bf_col=b_fgate[l][:, None], bf_row=b_fgate[l][None, :],
        qn_col=qn_g[l][:, None], kn_col=kn_g[l][:, None],
        qn_row=jnp.tile(qn_g[l], N_HEADS)[None, :], kn_row=jnp.tile(kn_g[l], N_HEADS)[None, :],
        a_re=ssm_a_re[l].reshape(1, N_STATE), a_im=ssm_a_im[l].reshape(1, N_STATE),
        ldt=jnp.repeat(ssm_log_dt[l], SSM_STATE)[None, :],
        bre=_block_diag(tr(ssm_b_re[l])).astype(BF16),
        bim=_block_diag(tr(ssm_b_im[l])).astype(BF16),
        cre=_block_diag(tr(ssm_c_re[l])).astype(BF16),
        cim=_block_diag(tr(ssm_c_im[l])).astype(BF16),
        d_row=ssm_d[l][None, :], wglu=w_glu[l].astype(BF16),
        w_br=jnp.stack([w_br_fox[l], w_br_sb[l], w_br_ssm[l]]).astype(BF16),
        w_out=w_out[l].astype(BF16))


def kernel(x_prompt, x_sample, cache_fox_k, cache_fox_v, cache_fox_logf, cache_sb_k, cache_sb_v, state_ssm_re, state_ssm_im, page_table, norm_g, w_in, b_fgate, qn_g, kn_g, ssm_a_re, ssm_a_im, ssm_log_dt, ssm_b_re, ssm_b_im, ssm_c_re, ssm_c_im, ssm_d, w_glu, w_br_fox, w_br_sb, w_br_ssm, w_out):
    depth = w_in.shape[0]
    nb, t, d = x_prompt.shape
    ns = x_sample.shape[0]
    to_pages = lambda c: jnp.transpose(c, (0, 1, 3, 4, 2))
    fkc, fvc, skc, svc = (to_pages(c) for c in (cache_fox_k, cache_fox_v, cache_sb_k, cache_sb_v))
    lfc = jnp.transpose(cache_fox_logf, (0, 1, 3, 2))
    headmean = _block_diag(jnp.full((N_HEADS, HEAD_DIM, HEAD_DIM), 1.0 / HEAD_DIM, F32))

    yp = x_prompt
    ys = x_sample.reshape(ns, d)
    ents_p, ents_s = [], []
    for l in range(depth):
        p = _prep_layer(l, norm_g, w_in, b_fgate, qn_g, kn_g, ssm_a_re, ssm_a_im, ssm_log_dt,
                        ssm_b_re, ssm_b_im, ssm_c_re, ssm_c_im, ssm_d, w_glu, w_br_fox,
                        w_br_sb, w_br_ssm, w_out)
        fq, fk, fv, lf, lfn, sq, sk, sv, u = _inproj_prompt(
            yp, p["g_row"], p["w_qkv"], p["w_ff"], p["bf_col"], p["bf_row"], p["qn_col"],
            p["kn_col"], tm=512)
        o_fox = _fox_prompt(fq, fk, fv, lf, lfn, tq=256)
        o_sb = _sb_prompt(sq, sk, sv, tq=256, tk=128)
        u_tm = jnp.swapaxes(u, 0, 1).reshape(t * nb, W_SSM)
        o_tm, hre, him = _ssm_prompt(u_tm, nb, p["a_re"], p["a_im"], p["ldt"], p["bre"],
                                     p["bim"], p["cre"], p["cim"], p["d_row"], p["wglu"], tt=128)
        o_ssm = jnp.swapaxes(o_tm.reshape(t, nb, W_SSM), 0, 1)
        yp = _merge(yp, o_fox, o_sb, o_ssm, p["g_row"], p["w_gate"], p["w_br"], p["w_out"],
                    tm=512, attn_transposed=True)
        ents_p.append((fk, fv, lf, sk, sv, hre, him))
        (dfq, dfk, dfv, dlf, dsq, dsk, dsv, dom, h1r, h1i) = _decode_pre(
            ys, p["g_row"], p["w_qkv"], p["w_ff"], p["bf_row"], p["qn_row"], p["kn_row"],
            headmean, p["a_re"], p["a_im"], p["ldt"], p["bre"], p["bim"], p["cre"], p["cim"],
            p["d_row"], p["wglu"], state_ssm_re[l].reshape(ns, N_STATE),
            state_ssm_im[l].reshape(ns, N_STATE))
        do_fox = _fox_decode(page_table, dfq[:, None, :], dfk[:, None, :], dfv[:, None, :],
                             dlf[:, :, None], fkc, fvc, lfc, l, g=8).reshape(ns, W_ATT)
        do_sb = _sb_decode(page_table, dsq[:, None, :], skc, svc, l, g=4).reshape(ns, W_ATT)
        ys = _merge(ys[None], do_fox[None], do_sb[None], dom[None], p["g_row"], p["w_gate"],
                    p["w_br"], p["w_out"], tm=ns, attn_transposed=False)[0]
        ents_s.append((dfk, dfv, dlf, dsk, dsv, h1r, h1i))

    def prompt_kv(i):
        a = jnp.stack([e[i] for e in ents_p]).reshape(depth, nb, N_HEADS, HEAD_DIM, t)
        return jnp.transpose(a, (0, 1, 4, 2, 3))

    p_logf = jnp.transpose(jnp.stack([e[2] for e in ents_p]), (0, 1, 3, 2))
    p_re = jnp.stack([e[5] for e in ents_p]).reshape(depth, nb, SSM_GROUPS, SSM_STATE)
    p_im = jnp.stack([e[6] for e in ents_p]).reshape(depth, nb, SSM_GROUPS, SSM_STATE)
    s_kv = lambda i: jnp.stack([e[i] for e in ents_s]).reshape(depth, ns, 1, N_HEADS, HEAD_DIM)
    s_logf = jnp.stack([e[2] for e in ents_s]).reshape(depth, ns, 1, N_HEADS)
    s_re = jnp.stack([e[5] for e in ents_s]).reshape(depth, ns, SSM_GROUPS, SSM_STATE)
    s_im = jnp.stack([e[6] for e in ents_s]).reshape(depth, ns, SSM_GROUPS, SSM_STATE)
    return (yp, ys.reshape(ns, 1, d), prompt_kv(0), prompt_kv(1), p_logf, prompt_kv(3),
            prompt_kv(4), p_re, p_im, s_kv(0), s_kv(1), s_logf, s_kv(3), s_kv(4), s_re, s_im)
```

```python
import functools
import math

import jax
import jax.numpy as jnp
from jax import lax
from jax.experimental import pallas as pl
from jax.experimental.pallas import tpu as pltpu

HEAD_DIM = 64
N_HEADS = 8
W_ATT = N_HEADS * HEAD_DIM
SSM_GROUP = 16
SSM_GROUPS = 32
SSM_STATE = 64
W_SSM = SSM_GROUP * SSM_GROUPS
N_STATE = SSM_GROUPS * SSM_STATE
EPS = 1e-6
NEG_INF = -1e30
SCALE = HEAD_DIM ** -0.5
SB_SKIP_LOG = -110.0
FOX_SKIP_LOG = -110.0
FOX_NEAR_STEPS = 2

LANES = 128
V7X_VMEM_BYTES = 64 * 1024 * 1024
VMEM_LIMIT = V7X_VMEM_BYTES - 8 * 1024 * 1024

F32 = jnp.float32
BF16 = jnp.bfloat16


def _cparams(*sem):
    return pltpu.CompilerParams(dimension_semantics=sem, vmem_limit_bytes=VMEM_LIMIT)


def _dot_nn(a, b, precision=None):
    return lax.dot_general(a, b, (((1,), (0,)), ((), ())), precision=precision,
                           preferred_element_type=F32)


def _dot_nt(a, b):
    return lax.dot_general(a, b, (((1,), (1,)), ((), ())), preferred_element_type=F32)


def _dot_tn(a, b):
    return lax.dot_general(a, b, (((0,), (0,)), ((), ())), preferred_element_type=F32)


def _log_sigmoid(x):
    return jnp.minimum(x, 0.0) - jnp.log(1.0 + jnp.exp(-jnp.abs(x)))


def _softplus(x):
    return jnp.maximum(x, 0.0) + jnp.log(1.0 + jnp.exp(-jnp.abs(x)))


def _sigmoid(x):
    return jax.nn.sigmoid(x)


def _gelu_tanh(x):
    c = math.sqrt(2.0 / math.pi)
    return x * (0.5 * (1.0 + jnp.tanh(c * (x + 0.044715 * (x * x * x)))))


def _rms_rows(x, g):
    ms = jnp.mean(x * x, axis=-1, keepdims=True)
    return x * lax.rsqrt(ms + EPS) * g


def _headnorm_t(xt, g_col):
    n = xt.shape[-1]
    x3 = xt.reshape(N_HEADS, HEAD_DIM, n)
    ms = jnp.mean(x3 * x3, axis=1, keepdims=True)
    y = x3 * lax.rsqrt(ms + EPS) * g_col[None]
    return y.reshape(W_ATT, n)


def _inproj_prompt_kernel(x_ref, g_ref, w_ref, wff_ref, bfc_ref, bfr_ref, qn_ref, kn_ref,
                          fq_ref, fk_ref, fv_ref, lf_ref, lfn_ref, sq_ref, sk_ref, sv_ref,
                          u_ref):
    h = _rms_rows(x_ref[...], g_ref[...]).astype(BF16)

    def proj_t(i):
        return _dot_nt(w_ref[i * W_ATT:(i + 1) * W_ATT, :], h)

    fq_ref[...] = (_headnorm_t(proj_t(0), qn_ref[...]) * SCALE).astype(BF16)
    fk_ref[...] = _headnorm_t(proj_t(1), kn_ref[...])
    fv_ref[...] = proj_t(2)
    wff = wff_ref[...]
    lf_ref[...] = _log_sigmoid(_dot_nt(wff, h)[:N_HEADS, :] + bfc_ref[...])
    lfn_ref[...] = _log_sigmoid(_dot_nt(h, wff)[:, :N_HEADS] + bfr_ref[...])
    sq_ref[...] = (proj_t(3) * SCALE).astype(BF16)
    sk_ref[...] = proj_t(4)
    sv_ref[...] = proj_t(5)
    u_ref[...] = _dot_nt(h, w_ref[6 * W_ATT:7 * W_ATT, :])


def _inproj_prompt(x, g_row, w_qkv, w_ff, bf_col, bf_row, qn_col, kn_col, tm):
    b, t, d = x.shape
    tm = min(tm, t)
    full = lambda a: pl.BlockSpec(a.shape, lambda i, j: (0,) * a.ndim)
    tspec = pl.BlockSpec((None, W_ATT, tm), lambda i, j: (i, 0, j))
    kv = jax.ShapeDtypeStruct((b, W_ATT, t), F32)
    qs = jax.ShapeDtypeStruct((b, W_ATT, t), BF16)
    return pl.pallas_call(
        _inproj_prompt_kernel,
        grid=(b, t // tm),
        in_specs=[pl.BlockSpec((None, tm, d), lambda i, j: (i, j, 0)), full(g_row), full(w_qkv),
                  full(w_ff), full(bf_col), full(bf_row), full(qn_col), full(kn_col)],
        out_specs=[tspec, tspec, tspec,
                   pl.BlockSpec((None, N_HEADS, tm), lambda i, j: (i, 0, j)),
                   pl.BlockSpec((None, tm, N_HEADS), lambda i, j: (i, j, 0)),
                   tspec, tspec, tspec,
                   pl.BlockSpec((None, tm, W_SSM), lambda i, j: (i, j, 0))],
        out_shape=[qs, kv, kv, jax.ShapeDtypeStruct((b, N_HEADS, t), F32),
                   jax.ShapeDtypeStruct((b, t, N_HEADS), F32), qs, kv, kv,
                   jax.ShapeDtypeStruct((b, t, W_SSM), F32)],
        compiler_params=_cparams("parallel", "parallel"),
        name="inproj_prompt",
    )(x, g_row, w_qkv, w_ff, bf_col, bf_row, qn_col, kn_col)


def _fox_prompt_kernel(q_ref, k_ref, v_ref, lf_ref, lfn_ref, o_ref,
                       kb_ref, vb_ref, crow_ref, cb_ref, acc_ref, m_ref, l_ref, a_ref, s_ref,
                       p_ref, *, tq, cb):
    qi = pl.program_id(1)
    t = k_ref.shape[-1]

    @pl.when(qi == 0)
    def _():
        kb_ref[...] = k_ref[...].astype(BF16)
        vb_ref[...] = v_ref[...].astype(BF16)
        r = lax.broadcasted_iota(jnp.int32, (cb, cb), 0)
        c = lax.broadcasted_iota(jnp.int32, (cb, cb), 1)
        upper = (c > r).astype(F32)
        lower = (r > c).astype(F32)
        carry_c = jnp.zeros((1, N_HEADS), F32)
        carry_r = jnp.zeros((N_HEADS, 1), F32)
        for blk in reversed(range(t // cb)):
            seg_c = lfn_ref[blk * cb:(blk + 1) * cb, :]
            within_c = _dot_nn(upper, seg_c, precision=lax.Precision.HIGHEST)
            c_col = -(within_c + carry_c)
            for h in range(N_HEADS):
                cb_ref[h, blk * cb:(blk + 1) * cb, :] = jnp.broadcast_to(c_col[:, h:h + 1],
                                                                         (cb, LANES))
            carry_c = carry_c + jnp.sum(seg_c, axis=0, keepdims=True)
            seg_r = lf_ref[:, blk * cb:(blk + 1) * cb]
            within_r = _dot_nn(seg_r, lower, precision=lax.Precision.HIGHEST)
            crow_ref[:, blk * cb:(blk + 1) * cb] = -(within_r + carry_r)
            carry_r = carry_r + jnp.sum(seg_r, axis=1, keepdims=True)

    m_ref[...] = jnp.full(m_ref.shape, NEG_INF, F32)
    l_ref[...] = jnp.zeros_like(l_ref)
    acc_ref[...] = jnp.zeros_like(acc_ref)
    q0 = pl.multiple_of(qi * tq, tq)

    def tile(j, masked):
        k0 = pl.multiple_of(j * tq, tq)
        heads = [slice(h * HEAD_DIM, (h + 1) * HEAD_DIM) for h in range(N_HEADS)]
        for h, rows in enumerate(heads):
            s_ref[h] = _dot_tn(kb_ref[rows, pl.ds(k0, tq)], q_ref[rows, :])
        for h in range(N_HEADS):
            ck = cb_ref[h, pl.ds(k0, tq), :]
            ck = jnp.concatenate([ck] * (tq // LANES), axis=1)
            s = s_ref[h] + (crow_ref[h:h + 1, pl.ds(q0, tq)] - ck)
            if masked:
                kpos = lax.broadcasted_iota(jnp.int32, (tq, tq), 0)
                qpos = lax.broadcasted_iota(jnp.int32, (tq, tq), 1)
                s = jnp.where(kpos <= qpos, s, NEG_INF)
            m_old = m_ref[h:h + 1, :]
            m_new = jnp.maximum(m_old, jnp.max(s, axis=0, keepdims=True))
            p = jnp.exp(s - m_new)
            alpha = jnp.exp(m_old - m_new)
            l_ref[h:h + 1, :] = alpha * l_ref[h:h + 1, :] + jnp.sum(p, axis=0, keepdims=True)
            m_ref[h:h + 1, :] = m_new
            a_ref[h:h + 1, :] = alpha
            p_ref[h] = p.astype(BF16)
        for h, rows in enumerate(heads):
            acc_ref[rows, :] = (a_ref[h:h + 1, :] * acc_ref[rows, :]
                                + _dot_nn(vb_ref[rows, pl.ds(k0, tq)], p_ref[h]))

    def body(j, carry):
        tile(j, False)
        return carry

    lax.fori_loop(0, qi, body, 0)
    tile(qi, True)
    for h in range(N_HEADS):
        rows = slice(h * HEAD_DIM, (h + 1) * HEAD_DIM)
        o_ref[rows, :] = acc_ref[rows, :] / l_ref[h:h + 1, :]


def _fox_prompt(q_t, k_t, v_t, lf_t, lf_n, tq):
    b, _, t = k_t.shape
    tq = min(tq, t)
    whole = pl.BlockSpec((None, W_ATT, t), lambda i, j: (i, 0, 0))
    return pl.pallas_call(
        functools.partial(_fox_prompt_kernel, tq=tq, cb=min(256, t)),
        grid=(b, t // tq),
        in_specs=[pl.BlockSpec((None, W_ATT, tq), lambda i, j: (i, 0, j)), whole, whole,
                  pl.BlockSpec((None, N_HEADS, t), lambda i, j: (i, 0, 0)),
                  pl.BlockSpec((None, t, N_HEADS), lambda i, j: (i, 0, 0))],
        out_specs=pl.BlockSpec((None, W_ATT, tq), lambda i, j: (i, 0, j)),
        out_shape=jax.ShapeDtypeStruct((b, W_ATT, t), F32),
        scratch_shapes=[pltpu.VMEM((W_ATT, t), BF16), pltpu.VMEM((W_ATT, t), BF16),
                        pltpu.VMEM((N_HEADS, t), F32), pltpu.VMEM((N_HEADS, t, LANES), F32),
                        pltpu.VMEM((W_ATT, tq), F32), pltpu.VMEM((N_HEADS, tq), F32),
                        pltpu.VMEM((N_HEADS, tq), F32), pltpu.VMEM((N_HEADS, tq), F32),
                        pltpu.VMEM((N_HEADS, tq, tq), F32), pltpu.VMEM((N_HEADS, tq, tq), BF16)],
        compiler_params=_cparams("parallel", "arbitrary"),
        name="fox_prompt",
    )(q_t, k_t, v_t, lf_t, lf_n)


def _sb_prompt_kernel(q_ref, k_ref, v_ref, o_ref, kb_ref, vb_ref, acc_ref, run_ref, sum_ref,
                      z_ref, in_ref, hi_ref, lo_ref, w_ref, *, tq, tk, skip_log):
    qi = pl.program_id(1)
    q0 = qi * tq
    nk = tq // tk

    @pl.when(qi == 0)
    def _():
        kb_ref[...] = k_ref[...].astype(BF16)
        vb_ref[...] = v_ref[...].astype(BF16)

    acc_ref[...] = jnp.zeros_like(acc_ref)
    run_ref[...] = jnp.zeros_like(run_ref)
    r = lax.broadcasted_iota(jnp.int32, (tk, tk), 0)
    c = lax.broadcasted_iota(jnp.int32, (tk, tk), 1)
    upper = (c > r).astype(BF16)

    def tile(kj, masked):
        k0 = pl.multiple_of(kj * tk, tk)
        heads = [slice(h * HEAD_DIM, (h + 1) * HEAD_DIM) for h in range(N_HEADS)]
        if masked:
            kpos = k0 + lax.broadcasted_iota(jnp.int32, (tk, tq), 0)
            qpos = q0 + lax.broadcasted_iota(jnp.int32, (tk, tq), 1)
            mask = kpos < qpos
        for h, rows in enumerate(heads):
            z_ref[h] = _dot_tn(kb_ref[rows, pl.ds(k0, tk)], q_ref[rows, :])
        for h in range(N_HEADS):
            z = z_ref[h]
            sp = _softplus(z)
            lk = jnp.where(mask, -sp, 0.0) if masked else -sp
            lk_hi = lk.astype(BF16)
            hi_ref[h] = lk_hi
            lo_ref[h] = (lk - lk_hi.astype(F32)).astype(BF16)
            z_ref[h] = z - sp
            sum_ref[h:h + 1, :] = jnp.sum(lk, axis=0, keepdims=True)
        for h in range(N_HEADS):
            in_ref[h] = _dot_nn(upper, hi_ref[h]) + _dot_nn(upper, lo_ref[h])
        for h in range(N_HEADS):
            run = run_ref[h:h + 1, :]
            w = jnp.exp(z_ref[h] + (in_ref[h] + run))
            if masked:
                w = jnp.where(mask, w, 0.0)
            w_ref[h] = w.astype(BF16)
            run_ref[h:h + 1, :] = run + sum_ref[h:h + 1, :]
        for h, rows in enumerate(heads):
            acc_ref[rows, :] += _dot_nn(vb_ref[rows, pl.ds(k0, tk)], w_ref[h])

    for j in range(nk):
        tile(qi * nk + (nk - 1) - j, True)

    def live():
        return (jnp.max(run_ref[...]) > skip_log).astype(jnp.int32)

    def cond(carry):
        kj, alive = carry
        return jnp.logical_and(kj >= 0, alive > 0)

    def body(carry):
        kj, _ = carry
        tile(kj, False)
        return kj - 1, live()

    lax.while_loop(cond, body, (qi * nk - 1, live()))
    o_ref[...] = acc_ref[...]


def _sb_prompt(q_t, k_t, v_t, tq, tk):
    b, _, t = k_t.shape
    tq = min(tq, t)
    tk = min(tk, tq)
    whole = pl.BlockSpec((None, W_ATT, t), lambda i, j: (i, 0, 0))
    return pl.pallas_call(
        functools.partial(_sb_prompt_kernel, tq=tq, tk=tk, skip_log=SB_SKIP_LOG),
        grid=(b, t // tq),
        in_specs=[pl.BlockSpec((None, W_ATT, tq), lambda i, j: (i, 0, j)), whole, whole],
        out_specs=pl.BlockSpec((None, W_ATT, tq), lambda i, j: (i, 0, j)),
        out_shape=jax.ShapeDtypeStruct((b, W_ATT, t), F32),
        scratch_shapes=[pltpu.VMEM((W_ATT, t), BF16), pltpu.VMEM((W_ATT, t), BF16),
                        pltpu.VMEM((W_ATT, tq), F32), pltpu.VMEM((N_HEADS, tq), F32),
                        pltpu.VMEM((N_HEADS, tq), F32),
                        pltpu.VMEM((N_HEADS, tk, tq), F32), pltpu.VMEM((N_HEADS, tk, tq), F32),
                        pltpu.VMEM((N_HEADS, tk, tq), BF16), pltpu.VMEM((N_HEADS, tk, tq), BF16),
                        pltpu.VMEM((N_HEADS, tk, tq), BF16)],
        compiler_params=_cparams("parallel", "arbitrary"),
        name="sb_prompt",
    )(q_t, k_t, v_t)


def _ssm_discretise(a_re, a_im, log_dt):
    dt = jnp.exp(log_dt)
    mag = jnp.exp(a_re * dt)
    ang = a_im * dt
    lb_re = mag * jnp.cos(ang)
    lb_im = mag * jnp.sin(ang)
    den = a_re * a_re + a_im * a_im
    nr = lb_re - 1.0
    coef_re = (nr * a_re + lb_im * a_im) / den
    coef_im = (lb_im * a_re - nr * a_im) / den
    return lb_re, lb_im, coef_re, coef_im


def _ssm_readout(h_re, h_im, u, cre_ref, cim_ref, d_row, wglu_ref):
    half_s, half_w = N_STATE // 2, W_SSM // 2
    hr = h_re.astype(BF16)
    hi = h_im.astype(BF16)
    y_lo = (_dot_nn(hr[:, :half_s], cre_ref[:half_s, :half_w])
            - _dot_nn(hi[:, :half_s], cim_ref[:half_s, :half_w]))
    y_hi = (_dot_nn(hr[:, half_s:], cre_ref[half_s:, half_w:])
            - _dot_nn(hi[:, half_s:], cim_ref[half_s:, half_w:]))
    y = jnp.concatenate([y_lo, y_hi], axis=1) + d_row * u
    y = _gelu_tanh(y)
    return y * _sigmoid(_dot_nn(y.astype(BF16), wglu_ref[...]))


def _ssm_prompt_kernel(u_ref, are_ref, aim_ref, ldt_ref, bre_ref, bim_ref, cre_ref, cim_ref,
                       d_ref, wglu_ref, o_ref, hre_ref, him_ref,
                       xre_ref, xim_ref, par_ref, hst_ref, *, nb, tt, lane_chunk):
    i = pl.program_id(0)

    @pl.when(i == 0)
    def _():
        pars = _ssm_discretise(are_ref[...], aim_ref[...], ldt_ref[...])
        for k in range(4):
            par_ref[k] = jnp.broadcast_to(pars[k], (nb, N_STATE))
        hst_ref[...] = jnp.zeros_like(hst_ref)

    ub = u_ref[...].astype(BF16)
    half_s, half_w = N_STATE // 2, W_SSM // 2
    xre_ref[:, :half_s] = _dot_nn(ub[:, :half_w], bre_ref[:half_w, :half_s])
    xre_ref[:, half_s:] = _dot_nn(ub[:, half_w:], bre_ref[half_w:, half_s:])
    xim_ref[:, :half_s] = _dot_nn(ub[:, :half_w], bim_ref[:half_w, :half_s])
    xim_ref[:, half_s:] = _dot_nn(ub[:, half_w:], bim_ref[half_w:, half_s:])

    for c in range(N_STATE // lane_chunk):
        cols = slice(c * lane_chunk, (c + 1) * lane_chunk)
        lb_re, lb_im = par_ref[0, :, cols], par_ref[1, :, cols]
        cf_re, cf_im = par_ref[2, :, cols], par_ref[3, :, cols]

        def body(t, carry):
            h_re, h_im = carry
            rws = pl.ds(pl.multiple_of(t * nb, nb), nb)
            b_re, b_im = xre_ref[rws, cols], xim_ref[rws, cols]
            x_re = cf_re * b_re - cf_im * b_im
            x_im = cf_re * b_im + cf_im * b_re
            n_re = lb_re * h_re - lb_im * h_im + x_re
            n_im = lb_re * h_im + lb_im * h_re + x_im
            xre_ref[rws, cols] = n_re
            xim_ref[rws, cols] = n_im
            return n_re, n_im

        h_re, h_im = lax.fori_loop(0, tt, body, (hst_ref[0, :, cols], hst_ref[1, :, cols]),
                                   unroll=8)
        hst_ref[0, :, cols] = h_re
        hst_ref[1, :, cols] = h_im

    o_ref[...] = _ssm_readout(xre_ref[...], xim_ref[...], u_ref[...], cre_ref, cim_ref,
                              d_ref[...], wglu_ref)
    hre_ref[...] = hst_ref[0]
    him_ref[...] = hst_ref[1]


def _ssm_prompt(u_tm, nb, a_re, a_im, ldt, bre, bim, cre, cim, d_row, wglu, tt):
    t = u_tm.shape[0] // nb
    tt = min(tt, t)
    full = lambda a: pl.BlockSpec(a.shape, lambda i: (0,) * a.ndim)
    st = jax.ShapeDtypeStruct((nb, N_STATE), F32)
    return pl.pallas_call(
        functools.partial(_ssm_prompt_kernel, nb=nb, tt=tt, lane_chunk=512),
        grid=(t // tt,),
        in_specs=[pl.BlockSpec((nb * tt, W_SSM), lambda i: (i, 0)), full(a_re), full(a_im),
                  full(ldt), full(bre), full(bim), full(cre), full(cim), full(d_row), full(wglu)],
        out_specs=[pl.BlockSpec((nb * tt, W_SSM), lambda i: (i, 0)),
                   pl.BlockSpec((nb, N_STATE), lambda i: (0, 0)),
                   pl.BlockSpec((nb, N_STATE), lambda i: (0, 0))],
        out_shape=[jax.ShapeDtypeStruct((t * nb, W_SSM), F32), st, st],
        scratch_shapes=[pltpu.VMEM((nb * tt, N_STATE), F32), pltpu.VMEM((nb * tt, N_STATE), F32),
                        pltpu.VMEM((4, nb, N_STATE), F32), pltpu.VMEM((2, nb, N_STATE), F32)],
        compiler_params=_cparams("arbitrary"),
        name="ssm_prompt",
    )(u_tm, a_re, a_im, ldt, bre, bim, cre, cim, d_row, wglu)


def _merge_kernel(x_ref, of_ref, os_ref, om_ref, g_ref, wg_ref, wbr_ref, wo_ref, y_ref,
                  *, attn_transposed):
    x = x_ref[...]
    h = _rms_rows(x, g_ref[...]).astype(BF16)

    def proj(lo, n):
        return _dot_nt(h, wg_ref[lo:lo + n, :])

    def branch(o, z_lo, wi, g_lo):
        z = proj(z_lo, W_ATT)
        a = (o * (z * _sigmoid(z))).astype(BF16)
        br = _dot_nn(a, wbr_ref[wi])
        return _sigmoid(proj(g_lo, x.shape[-1])) * br

    of = of_ref[...].T if attn_transposed else of_ref[...]
    os_ = os_ref[...].T if attn_transposed else os_ref[...]
    d = x.shape[-1]
    m = (branch(of, 0, 0, 3 * W_ATT) + branch(os_, W_ATT, 1, 3 * W_ATT + d)
         + branch(om_ref[...], 2 * W_ATT, 2, 3 * W_ATT + 2 * d))
    y_ref[...] = x + _dot_nn(m.astype(BF16), wo_ref[...])


def _merge(x, of, os_, om, g_row, w_gate, w_br, w_out, tm, attn_transposed):
    b, t, d = x.shape
    tm = min(tm, t)
    full = lambda a: pl.BlockSpec(a.shape, lambda i, j: (0,) * a.ndim)
    rspec = lambda n: pl.BlockSpec((None, tm, n), lambda i, j: (i, j, 0))
    aspec = pl.BlockSpec((None, W_ATT, tm), lambda i, j: (i, 0, j)) if attn_transposed \
        else rspec(W_ATT)
    return pl.pallas_call(
        functools.partial(_merge_kernel, attn_transposed=attn_transposed),
        grid=(b, t // tm),
        in_specs=[rspec(d), aspec, aspec, rspec(W_SSM), full(g_row), full(w_gate), full(w_br),
                  full(w_out)],
        out_specs=rspec(d),
        out_shape=jax.ShapeDtypeStruct((b, t, d), F32),
        compiler_params=_cparams("parallel", "parallel"),
        name="merge_t" if attn_transposed else "merge_n",
    )(x, of, os_, om, g_row, w_gate, w_br, w_out)


def _decode_pre_kernel(x_ref, g_ref, w_ref, wff_ref, bfr_ref, qn_ref, kn_ref, hd_ref,
                       are_ref, aim_ref, ldt_ref, bre_ref, bim_ref, cre_ref, cim_ref, d_ref,
                       wglu_ref, h0r_ref, h0i_ref,
                       fq_ref, fk_ref, fv_ref, lf_ref, sq_ref, sk_ref, sv_ref, om_ref,
                       h1r_ref, h1i_ref):
    h = _rms_rows(x_ref[...], g_ref[...]).astype(BF16)

    def proj(i):
        return _dot_nt(h, w_ref[i * W_ATT:(i + 1) * W_ATT, :])

    def headnorm(x, g_row):
        ms = _dot_nn(x * x, hd_ref[...], precision=lax.Precision.HIGHEST)
        return x * lax.rsqrt(ms + EPS) * g_row

    fq_ref[...] = headnorm(proj(0), qn_ref[...]) * SCALE
    fk_ref[...] = headnorm(proj(1), kn_ref[...])
    fv_ref[...] = proj(2)
    lf_ref[...] = _log_sigmoid(_dot_nt(h, wff_ref[...])[:, :N_HEADS] + bfr_ref[...])
    sq_ref[...] = proj(3) * SCALE
    sk_ref[...] = proj(4)
    sv_ref[...] = proj(5)
    u = proj(6)
    lb_re, lb_im, cf_re, cf_im = _ssm_discretise(are_ref[...], aim_ref[...], ldt_ref[...])
    ub = u.astype(BF16)
    b_re = _dot_nn(ub, bre_ref[...])
    b_im = _dot_nn(ub, bim_ref[...])
    x_re = cf_re * b_re - cf_im * b_im
    x_im = cf_re * b_im + cf_im * b_re
    r0, i0 = h0r_ref[...], h0i_ref[...]
    h_re = x_re + lb_re * r0 - lb_im * i0
    h_im = x_im + lb_re * i0 + lb_im * r0
    h1r_ref[...] = h_re
    h1i_ref[...] = h_im
    om_ref[...] = _ssm_readout(h_re, h_im, u, cre_ref, cim_ref, d_ref[...], wglu_ref)


def _decode_pre(x, g_row, w_qkv, w_ff, bf_row, qn_row, kn_row, headmean, a_re, a_im, ldt,
                bre, bim, cre, cim, d_row, wglu, h0r, h0i):
    n = x.shape[0]
    att = jax.ShapeDtypeStruct((n, W_ATT), F32)
    st = jax.ShapeDtypeStruct((n, N_STATE), F32)
    return pl.pallas_call(
        _decode_pre_kernel,
        out_shape=[att, att, att, jax.ShapeDtypeStruct((n, N_HEADS), F32), att, att, att,
                   jax.ShapeDtypeStruct((n, W_SSM), F32), st, st],
        compiler_params=pltpu.CompilerParams(vmem_limit_bytes=VMEM_LIMIT),
        name="decode_pre",
    )(x, g_row, w_qkv, w_ff, bf_row, qn_row, kn_row, headmean, a_re, a_im, ldt, bre, bim,
      cre, cim, d_row, wglu, h0r, h0i)


def _suffix_sums(blocks):
    x = jnp.concatenate(blocks, axis=0)
    r = lax.broadcasted_iota(jnp.int32, (LANES, LANES), 0)
    c = lax.broadcasted_iota(jnp.int32, (LANES, LANES), 1)
    y = _dot_nn(x, (r > c).astype(F32), precision=lax.Precision.HIGHEST)
    return [y[i * N_HEADS:(i + 1) * N_HEADS] for i in range(len(blocks))]


def _head_block_rows(row):
    head = lax.broadcasted_iota(jnp.int32, (N_HEADS, W_ATT), 0)
    lane = lax.broadcasted_iota(jnp.int32, (N_HEADS, W_ATT), 1)
    return jnp.where(lane // HEAD_DIM == head, jnp.broadcast_to(row, (N_HEADS, W_ATT)), 0.0)


def _page_scores(qblk, k_ref):
    return _dot_nn(qblk.astype(BF16), k_ref[...].reshape(W_ATT, k_ref.shape[-1]).astype(BF16))


def _weights_times_pages(weights, v_refs):
    page = v_refs[0].shape[-1]
    v_all = jnp.concatenate([v[...].reshape(W_ATT, page).astype(BF16) for v in v_refs], axis=1)
    w_all = jnp.concatenate(weights, axis=1).astype(BF16)
    return _dot_nt(w_all, v_all)


def _head_rows_to_row(x):
    head = lax.broadcasted_iota(jnp.int32, x.shape, 0)
    lane = lax.broadcasted_iota(jnp.int32, x.shape, 1)
    return jnp.sum(jnp.where(lane // HEAD_DIM == head, x, 0.0), axis=0, keepdims=True)


def _per_head_to_row(col):
    return _head_rows_to_row(jnp.broadcast_to(col, (N_HEADS, W_ATT)))


def _fox_scores_kernel(pt_ref, q_ref, kn_ref, lfn_ref, *refs, g, near_steps):
    k_refs, lf_refs = refs[:g], refs[g:2 * g]
    s_ref, m_ref, mfar_ref, qblk_ref, run_ref = refs[2 * g:]
    j = pl.program_id(1)
    page = k_refs[0].shape[-1]

    @pl.when(j == 0)
    def _():
        qblk = _head_block_rows(q_ref[...])
        qblk_ref[...] = qblk
        m_ref[...] = jnp.sum(qblk * kn_ref[...], axis=1, keepdims=True)
        mfar_ref[...] = jnp.full(mfar_ref.shape, NEG_INF, F32)
        run_ref[...] = lfn_ref[...]

    lfs = [lf_refs[i][...] for i in range(g)]
    suffix = _suffix_sums(lfs)
    run = run_ref[...]
    m_step = jnp.full(m_ref.shape, NEG_INF, F32)
    for i in range(g):
        s = _page_scores(qblk_ref[...], k_refs[i]) + (run + suffix[i])
        run = run + jnp.sum(lfs[i], axis=1, keepdims=True)
        m_step = jnp.maximum(m_step, jnp.max(s, axis=1, keepdims=True))
        s_ref[:, i * page:(i + 1) * page] = s
    run_ref[...] = run
    m_ref[...] = jnp.maximum(m_ref[...], m_step)

    @pl.when(j >= near_steps)
    def _():
        mfar_ref[...] = jnp.maximum(mfar_ref[...], m_step)


def _fox_values_kernel(pt_ref, q_ref, kn_ref, vn_ref, s_ref, m_ref, *refs, g):
    v_refs = refs[:g]
    o_ref, acc_ref, l_ref = refs[g:]
    j = pl.program_id(1)
    page = v_refs[0].shape[-1]
    m = m_ref[...]

    @pl.when(j == 0)
    def _():
        s_self = jnp.sum(_head_block_rows(q_ref[...]) * kn_ref[...], axis=1, keepdims=True)
        w_self = jnp.exp(s_self - m)
        l_ref[...] = w_self
        acc_ref[...] = w_self * jnp.broadcast_to(vn_ref[...], acc_ref.shape)

    probs = [jnp.exp(s_ref[:, i * page:(i + 1) * page] - m) for i in range(g)]
    l_new = l_ref[...]
    for p in probs:
        l_new = l_new + jnp.sum(p, axis=1, keepdims=True)
    l_ref[...] = l_new
    acc_ref[...] += _weights_times_pages(probs, v_refs)

    @pl.when(j == pl.num_programs(1) - 1)
    def _():
        o_ref[...] = _head_rows_to_row(acc_ref[...]) / _per_head_to_row(l_ref[...])


def _sb_decode_kernel(pt_ref, q_ref, run0_ref, *refs, g):
    k_refs, v_refs = refs[:g], refs[g:2 * g]
    o_ref, runo_ref, qblk_ref, acc_ref, run_ref = refs[2 * g:]
    j = pl.program_id(1)

    @pl.when(j == 0)
    def _():
        qblk_ref[...] = _head_block_rows(q_ref[...])
        acc_ref[...] = jnp.zeros_like(acc_ref)
        run_ref[...] = run0_ref[...]

    zs = [_page_scores(qblk_ref[...], k_refs[i]) for i in range(g)]
    sps = [_softplus(z) for z in zs]
    suffix = _suffix_sums([-sp for sp in sps])
    run = run_ref[...]
    weights = []
    for i in range(g):
        weights.append(jnp.exp((zs[i] - sps[i]) + (suffix[i] + run)))
        run = run - jnp.sum(sps[i], axis=1, keepdims=True)
    run_ref[...] = run
    acc_ref[...] += _weights_times_pages(weights, v_refs)

    @pl.when(j == pl.num_programs(1) - 1)
    def _():
        o_ref[...] = _head_rows_to_row(acc_ref[...])
        runo_ref[...] = run_ref[...]


def _page_spec(layer, last_page, g, i, tail, step_of=lambda j: j):
    def index(b, j, pt):
        return (layer, pt[b, last_page - (step_of(j) * g + i)]) + (0,) * len(tail)
    return pl.BlockSpec((None, None) + tail, index)


def _fox_decode(page_table, q_row, k_new_row, v_new_row, lf_new, kc, vc, lfc, layer, g):
    n, n_pages = page_table.shape
    page = kc.shape[-1]
    g = min(g, n_pages)
    steps = n_pages // g
    near_steps = min(FOX_NEAR_STEPS, steps)
    row = pl.BlockSpec((None, 1, W_ATT), lambda b, j, pt: (b, 0, 0))
    hcol = pl.BlockSpec((None, N_HEADS, 1), lambda b, j, pt: (b, 0, 0))
    sblk = pl.BlockSpec((None, N_HEADS, g * page), lambda b, j, pt: (b, 0, j))
    kv_tail = (N_HEADS, HEAD_DIM, page)
    last = n_pages - 1
    stat = jax.ShapeDtypeStruct((n, N_HEADS, 1), F32)
    s, m, m_far = pl.pallas_call(
        functools.partial(_fox_scores_kernel, g=g, near_steps=near_steps),
        grid_spec=pltpu.PrefetchScalarGridSpec(
            num_scalar_prefetch=1, grid=(n, steps),
            in_specs=([row, row, hcol]
                      + [_page_spec(layer, last, g, i, kv_tail) for i in range(g)]
                      + [_page_spec(layer, last, g, i, (N_HEADS, page)) for i in range(g)]),
            out_specs=[sblk, hcol, hcol],
            scratch_shapes=[pltpu.VMEM((N_HEADS, W_ATT), F32), pltpu.VMEM((N_HEADS, 1), F32)]),
        out_shape=[jax.ShapeDtypeStruct((n, N_HEADS, n_pages * page), F32), stat, stat],
        compiler_params=_cparams("parallel", "arbitrary"),
        name="fox_scores",
    )(page_table, q_row, k_new_row, lf_new, *([kc] * g), *([lfc] * g))

    def values(n_steps):
        return pl.pallas_call(
            functools.partial(_fox_values_kernel, g=g),
            grid_spec=pltpu.PrefetchScalarGridSpec(
                num_scalar_prefetch=1, grid=(n, n_steps),
                in_specs=([row, row, row, sblk, hcol]
                          + [_page_spec(layer, last, g, i, kv_tail) for i in range(g)]),
                out_specs=row,
                scratch_shapes=[pltpu.VMEM((N_HEADS, W_ATT), F32),
                                pltpu.VMEM((N_HEADS, 1), F32)]),
            out_shape=jax.ShapeDtypeStruct((n, 1, W_ATT), F32),
            compiler_params=_cparams("parallel", "arbitrary"),
            name="fox_values",
        )(page_table, q_row, k_new_row, v_new_row, s, m, *([vc] * g))

    if near_steps == steps:
        return values(steps)
    return lax.cond(jnp.max(m_far - m) > FOX_SKIP_LOG, lambda: values(steps),
                    lambda: values(near_steps))


def _sb_decode_pages(page_table, q, run0, kc, vc, layer, g, first, count):
    n = page_table.shape[0]
    page = kc.shape[-1]
    row = pl.BlockSpec((None, 1, W_ATT), lambda b, j, pt: (b, 0, 0))
    hcol = pl.BlockSpec((None, N_HEADS, 1), lambda b, j, pt: (b, 0, 0))
    kv_tail = (N_HEADS, HEAD_DIM, page)
    last = first + count - 1
    in_specs = ([row, hcol] + [_page_spec(layer, last, g, i, kv_tail) for i in range(g)]
                + [_page_spec(layer, last, g, i, kv_tail) for i in range(g)])
    return pl.pallas_call(
        functools.partial(_sb_decode_kernel, g=g),
        grid_spec=pltpu.PrefetchScalarGridSpec(
            num_scalar_prefetch=1, grid=(n, count // g), in_specs=in_specs,
            out_specs=[row, hcol],
            scratch_shapes=[pltpu.VMEM((N_HEADS, W_ATT), F32),
                            pltpu.VMEM((N_HEADS, W_ATT), F32),
                            pltpu.VMEM((N_HEADS, 1), F32)]),
        out_shape=[jax.ShapeDtypeStruct((n, 1, W_ATT), F32),
                   jax.ShapeDtypeStruct((n, N_HEADS, 1), F32)],
        compiler_params=_cparams("parallel", "arbitrary"),
        name="sb_decode",
    )(page_table, q, run0, *([kc] * g), *([vc] * g))


def _sb_decode(page_table, q, kc, vc, layer, g):
    n, n_pages = page_table.shape
    near = min(g, n_pages)
    run0 = jnp.zeros((n, N_HEADS, 1), F32)
    o_near, run = _sb_decode_pages(page_table, q, run0, kc, vc, layer, near, n_pages - near, near)
    if near == n_pages:
        return o_near

    def rest():
        return o_near + _sb_decode_pages(page_table, q, run, kc, vc, layer, g, 0,
                                         n_pages - near)[0]

    return lax.cond(jnp.max(run) > SB_SKIP_LOG, rest, lambda: o_near)


def _block_diag(blocks):
    g, r, c = blocks.shape
    eye = jnp.eye(g, dtype=blocks.dtype)
    return jnp.einsum("grc,gh->grhc", blocks, eye).reshape(g * r, g * c)


def _prep_layer(l, norm_g, w_in, b_fgate, qn_g, kn_g, ssm_a_re, ssm_a_im, ssm_log_dt,
                ssm_b_re, ssm_b_im, ssm_c_re, ssm_c_im, ssm_d, w_glu, w_br_fox, w_br_sb,
                w_br_ssm, w_out):
    d = w_in.shape[1]
    wt = jnp.swapaxes(w_in[l], 0, 1)
    off = [0]
    for s in (W_ATT, W_ATT, W_ATT, N_HEADS, W_ATT, W_ATT, W_ATT, W_ATT, W_ATT, W_SSM, W_SSM,
              d, d, d):
        off.append(off[-1] + s)
    seg = lambda i: wt[off[i]:off[i + 1]]
    w_qkv = jnp.concatenate([seg(0), seg(1), seg(2), seg(5), seg(6), seg(7), seg(9)]).astype(BF16)
    w_ff = jnp.concatenate([seg(3), jnp.zeros((LANES - N_HEADS, d), F32)]).astype(BF16)
    w_gate = jnp.concatenate([seg(4), seg(8), seg(10), seg(11), seg(12), seg(13)]).astype(BF16)
    tr = lambda a: jnp.swapaxes(a, 1, 2)
    return dict(
        g_row=norm_g[l][None, :], w_qkv=w_qkv, w_ff=w_ff, w_gate=w_gate,
        bf_col=b_fgate[l][:, None], bf_row=b_fgate[l][None, :],
        qn_col=qn_g[l][:, None], kn_col=kn_g[l][:, None],
        qn_row=jnp.tile(qn_g[l], N_HEADS)[None, :], kn_row=jnp.tile(kn_g[l], N_HEADS)[None, :],
        a_re=ssm_a_re[l].reshape(1, N_STATE), a_im=ssm_a_im[l].reshape(1, N_STATE),
        ldt=jnp.repeat(ssm_log_dt[l], SSM_STATE)[None, :],
        bre=_block_diag(tr(ssm_b_re[l])).astype(BF16),
        bim=_block_diag(tr(ssm_b_im[l])).astype(BF16),
        cre=_block_diag(tr(ssm_c_re[l])).astype(BF16),
        cim=_block_diag(tr(ssm_c_im[l])).astype(BF16),
        d_row=ssm_d[l][None, :], wglu=w_glu[l].astype(BF16),
        w_br=jnp.stack([w_br_fox[l], w_br_sb[l], w_br_ssm[l]]).astype(BF16),
        w_out=w_out[l].astype(BF16))


def kernel(x_prompt, x_sample, cache_fox_k, cache_fox_v, cache_fox_logf, cache_sb_k, cache_sb_v, state_ssm_re, state_ssm_im, page_table, norm_g, w_in, b_fgate, qn_g, kn_g, ssm_a_re, ssm_a_im, ssm_log_dt, ssm_b_re, ssm_b_im, ssm_c_re, ssm_c_im, ssm_d, w_glu, w_br_fox, w_br_sb, w_br_ssm, w_out):
    depth = w_in.shape[0]
    nb, t, d = x_prompt.shape
    ns = x_sample.shape[0]
    to_pages = lambda c: jnp.transpose(c, (0, 1, 3, 4, 2))
    fkc, fvc, skc, svc = (to_pages(c) for c in (cache_fox_k, cache_fox_v, cache_sb_k, cache_sb_v))
    lfc = jnp.transpose(cache_fox_logf, (0, 1, 3, 2))
    headmean = _block_diag(jnp.full((N_HEADS, HEAD_DIM, HEAD_DIM), 1.0 / HEAD_DIM, F32))

    yp = x_prompt
    ys = x_sample.reshape(ns, d)
    ents_p, ents_s = [], []
    for l in range(depth):
        p = _prep_layer(l, norm_g, w_in, b_fgate, qn_g, kn_g, ssm_a_re, ssm_a_im, ssm_log_dt,
                        ssm_b_re, ssm_b_im, ssm_c_re, ssm_c_im, ssm_d, w_glu, w_br_fox,
                        w_br_sb, w_br_ssm, w_out)
        fq, fk, fv, lf, lfn, sq, sk, sv, u = _inproj_prompt(
            yp, p["g_row"], p["w_qkv"], p["w_ff"], p["bf_col"], p["bf_row"], p["qn_col"],
            p["kn_col"], tm=512)
        o_fox = _fox_prompt(fq, fk, fv, lf, lfn, tq=256)
        o_sb = _sb_prompt(sq, sk, sv, tq=256, tk=128)
        u_tm = jnp.swapaxes(u, 0, 1).reshape(t * nb, W_SSM)
        o_tm, hre, him = _ssm_prompt(u_tm, nb, p["a_re"], p["a_im"], p["ldt"], p["bre"],
                                     p["bim"], p["cre"], p["cim"], p["d_row"], p["wglu"], tt=128)
        o_ssm = jnp.swapaxes(o_tm.reshape(t, nb, W_SSM), 0, 1)
        yp = _merge(yp, o_fox, o_sb, o_ssm, p["g_row"], p["w_gate"], p["w_br"], p["w_out"],
                    tm=512, attn_transposed=True)
        ents_p.append((fk, fv, lf, sk, sv, hre, him))
        (dfq, dfk, dfv, dlf, dsq, dsk, dsv, dom, h1r, h1i) = _decode_pre(
            ys, p["g_row"], p["w_qkv"], p["w_ff"], p["bf_row"], p["qn_row"], p["kn_row"],
            headmean, p["a_re"], p["a_im"], p["ldt"], p["bre"], p["bim"], p["cre"], p["cim"],
            p["d_row"], p["wglu"], state_ssm_re[l].reshape(ns, N_STATE),
            state_ssm_im[l].reshape(ns, N_STATE))
        do_fox = _fox_decode(page_table, dfq[:, None, :], dfk[:, None, :], dfv[:, None, :],
                             dlf[:, :, None], fkc, fvc, lfc, l, g=16).reshape(ns, W_ATT)
        do_sb = _sb_decode(page_table, dsq[:, None, :], skc, svc, l, g=4).reshape(ns, W_ATT)
        ys = _merge(ys[None], do_fox[None], do_sb[None], dom[None], p["g_row"], p["w_gate"],
                    p["w_br"], p["w_out"], tm=ns, attn_transposed=False)[0]
        ents_s.append((dfk, dfv, dlf, dsk, dsv, h1r, h1i))

    def prompt_kv(i):
        a = jnp.stack([e[i] for e in ents_p]).reshape(depth, nb, N_HEADS, HEAD_DIM, t)
        return jnp.transpose(a, (0, 1, 4, 2, 3))

    p_logf = jnp.transpose(jnp.stack([e[2] for e in ents_p]), (0, 1, 3, 2))
    p_re = jnp.stack([e[5] for e in ents_p]).reshape(depth, nb, SSM_GROUPS, SSM_STATE)
    p_im = jnp.stack([e[6] for e in ents_p]).reshape(depth, nb, SSM_GROUPS, SSM_STATE)
    s_kv = lambda i: jnp.stack([e[i] for e in ents_s]).reshape(depth, ns, 1, N_HEADS, HEAD_DIM)
    s_logf = jnp.stack([e[2] for e in ents_s]).reshape(depth, ns, 1, N_HEADS)
    s_re = jnp.stack([e[5] for e in ents_s]).reshape(depth, ns, SSM_GROUPS, SSM_STATE)
    s_im = jnp.stack([e[6] for e in ents_s]).reshape(depth, ns, SSM_GROUPS, SSM_STATE)
    return (yp, ys.reshape(ns, 1, d), prompt_kv(0), prompt_kv(1), p_logf, prompt_kv(3),
            prompt_kv(4), p_re, p_im, s_kv(0), s_kv(1), s_logf, s_kv(3), s_kv(4), s_re, s_im)
```
